```python
import math
import jax, jax.numpy as jnp
from jax import lax
import numpy as np

D_MODEL = 1024
BATCH = 8
SEQ = 2048
DEPTH = 2
DEC_BATCH = 128
DEC_SEQ = 1
PAST_LEN = 2048
PAGE_SIZE = 128

N_META = 16
HEAD_DIM = 64
N_FOX_HEADS = 8
N_SB_HEADS = 8
FOX_W = N_FOX_HEADS * HEAD_DIM
SB_W = N_SB_HEADS * HEAD_DIM
SSM_GROUP = 16
SSM_W = D_MODEL // 2
N_SSM_GROUPS = SSM_W // SSM_GROUP
SSM_STATE = 64
N_BRANCH = 3
D_FF = math.ceil(8 * D_MODEL / 3 / 256) * 256
Q_BLOCK = 128
LN_EPS = 1e-5
DN_ALPHA = (2 * DEPTH) ** 0.25
DN_BETA = (8 * DEPTH) ** -0.25
DT_MIN = 1e-3
DT_MAX = 1e-1

OFF_FOX_QKV = SSM_W
OFF_FOX_F = OFF_FOX_QKV + 3 * FOX_W
OFF_SB_QKV = OFF_FOX_F + N_FOX_HEADS
OFF_GATE = OFF_SB_QKV + 3 * SB_W
IN_W = OFF_GATE + N_BRANCH * D_MODEL

kernel_name = "hybrid_s5_fox_stickbreak_step"


def layer_norm(x, g, b):
    xf = x.astype(jnp.float32)
    mu = xf.mean(-1, keepdims=True)
    var = jnp.square(xf - mu).mean(-1, keepdims=True)
    return ((xf - mu) * lax.rsqrt(var + LN_EPS) * g + b).astype(x.dtype)


def zoh(a_re, a_im, log_dt, b_re, b_im):
    f32 = jnp.float32
    dt = jnp.exp(log_dt.astype(f32))[:, None]
    ar, ai = a_re.astype(f32), a_im.astype(f32)
    mag = jnp.exp(dt * ar)
    abar_re, abar_im = mag * jnp.cos(dt * ai), mag * jnp.sin(dt * ai)
    nr, ni = abar_re - 1.0, abar_im
    den = ar * ar + ai * ai
    q_re = (nr * ar + ni * ai) / den
    q_im = (ni * ar - nr * ai) / den
    br, bi = b_re.astype(f32), b_im.astype(f32)
    bb_re = q_re[..., None] * br - q_im[..., None] * bi
    bb_im = q_re[..., None] * bi + q_im[..., None] * br
    return abar_re, abar_im, bb_re, bb_im


def _cplx_affine_combine(e1, e2):
    a1r, a1i, b1r, b1i = e1
    a2r, a2i, b2r, b2i = e2
    return (a2r * a1r - a2i * a1i, a2r * a1i + a2i * a1r,
            a2r * b1r - a2i * b1i + b2r, a2r * b1i + a2i * b1r + b2i)


def ssm_branch(u, h0_re, h0_im, a_re, a_im, log_dt, b_re, b_im, c_re, c_im, d, w_glu):
    bsz, n = u.shape[:2]
    abar_re, abar_im, bb_re, bb_im = zoh(a_re, a_im, log_dt, b_re, b_im)
    ug = u.astype(jnp.float32).reshape(bsz, n, N_SSM_GROUPS, SSM_GROUP)
    bu_re = jnp.einsum('bngc,gpc->bngp', ug, bb_re)
    bu_im = jnp.einsum('bngc,gpc->bngp', ug, bb_im)
    h0r, h0i = h0_re.astype(jnp.float32), h0_im.astype(jnp.float32)
    bu_re = bu_re.at[:, 0].add(abar_re * h0r - abar_im * h0i)
    bu_im = bu_im.at[:, 0].add(abar_re * h0i + abar_im * h0r)
    a_r = jnp.broadcast_to(abar_re, bu_re.shape)
    a_i = jnp.broadcast_to(abar_im, bu_re.shape)
    _, _, hr, hi = lax.associative_scan(_cplx_affine_combine, (a_r, a_i, bu_re, bu_im), axis=1)
    y = (jnp.einsum('bngp,gcp->bngc', hr, c_re.astype(jnp.float32))
         - jnp.einsum('bngp,gcp->bngc', hi, c_im.astype(jnp.float32))
         + d.astype(jnp.float32) * ug)
    y = jax.nn.gelu(y.reshape(bsz, n, SSM_W)).astype(u.dtype)
    ab = y @ w_glu
    out = ab[..., :D_MODEL] * jax.nn.sigmoid(ab[..., D_MODEL:])
    return out, hr[:, -1], hi[:, -1]


def fox_attend(q_side, q_pos, kv_side, k_pos):
    q, fq = q_side
    k, v, fk = kv_side
    s = jnp.einsum('bqhd,bkhd->bhqk', q, k, preferred_element_type=jnp.float32) * HEAD_DIM ** -0.5
    s = s + (jnp.swapaxes(fq, 1, 2)[..., :, None] - jnp.swapaxes(fk, 1, 2)[..., None, :])
    causal = k_pos[None, :] <= q_pos[:, None]
    p = jax.nn.softmax(jnp.where(causal, s, -jnp.inf), axis=-1)
    return jnp.einsum('bhqk,bkhd->bqhd', p.astype(v.dtype), v)


def sb_attend(q_side, q_pos, kv_side, k_pos):
    (q,) = q_side
    k, v = kv_side
    z = jnp.einsum('bqhd,bkhd->bhqk', q, k, preferred_element_type=jnp.float32) * HEAD_DIM ** -0.5
    visible = k_pos[None, :] < q_pos[:, None]
    log_1m = jnp.where(visible, jax.nn.log_sigmoid(-z), 0.0)
    later = lax.cumsum(log_1m, axis=3, reverse=True) - log_1m
    w = jnp.where(visible, jnp.exp(jax.nn.log_sigmoid(z) + later), 0.0)
    return jnp.einsum('bhqk,bkhd->bqhd', w.astype(v.dtype), v)


def causal_blocks(attend, q_side, kv_side):
    bsz, n = q_side[0].shape[:2]
    meta_pos = jnp.arange(N_META)
    meta = attend(tuple(a[:, :N_META] for a in q_side), meta_pos,
                  tuple(a[:, :N_META] for a in kv_side), meta_pos)
    n_blocks = (n - N_META) // Q_BLOCK
    k_pos = jnp.arange(n)

    def one(i):
        start = N_META + i * Q_BLOCK
        qs = tuple(lax.dynamic_slice_in_dim(a, start, Q_BLOCK, axis=1) for a in q_side)
        return attend(qs, start + jnp.arange(Q_BLOCK), kv_side, k_pos)

    blocks = lax.map(one, jnp.arange(n_blocks))
    real = jnp.swapaxes(blocks, 0, 1)
    real = real.reshape((bsz, n_blocks * Q_BLOCK) + real.shape[3:])
    return jnp.concatenate([meta, real], axis=1)


def paged_past(cache, l, page_table):
    rows = cache[l, page_table]
    return rows.reshape((rows.shape[0], -1) + rows.shape[3:])


def run_layer(h, h0_re, h0_im, fox_mix, sb_mix, lp):
    (w_in, b_forget, a_re, a_im, log_dt, b_re, b_im, c_re, c_im, d, w_glu,
     w_fox_o, w_sb_o, w_out, ln1_g, ln1_b, w_ffn_in, w_ffn_out, ln2_g, ln2_b) = lp
    bsz, n = h.shape[:2]
    z = h @ w_in
    u = z[..., :SSM_W]
    fox = z[..., OFF_FOX_QKV:OFF_FOX_F].reshape(bsz, n, 3, N_FOX_HEADS, HEAD_DIM)
    logf = jax.nn.log_sigmoid((z[..., OFF_FOX_F:OFF_SB_QKV] + b_forget).astype(jnp.float32))
    sb = z[..., OFF_SB_QKV:OFF_GATE].reshape(bsz, n, 3, N_SB_HEADS, HEAD_DIM)
    gates = jax.nn.sigmoid(z[..., OFF_GATE:]).reshape(bsz, n, N_BRANCH, D_MODEL)
    fq, fk, fv = fox[:, :, 0], fox[:, :, 1], fox[:, :, 2]
    sq, sk, sv = sb[:, :, 0], sb[:, :, 1], sb[:, :, 2]

    y_ssm, hr, hi = ssm_branch(u, h0_re, h0_im, a_re, a_im, log_dt, b_re, b_im, c_re, c_im, d, w_glu)
    o_fox = fox_mix(fq, fk, fv, logf).reshape(bsz, n, FOX_W) @ w_fox_o
    o_sb = sb_mix(sq, sk, sv).reshape(bsz, n, SB_W) @ w_sb_o
    mix = (gates[:, :, 0] * y_ssm + gates[:, :, 1] * o_fox + gates[:, :, 2] * o_sb) @ w_out
    h = layer_norm(DN_ALPHA * h + mix, ln1_g, ln1_b)

    gu = h @ w_ffn_in
    ffn = (jax.nn.silu(gu[..., :D_FF]) * gu[..., D_FF:]) @ w_ffn_out
    h = layer_norm(DN_ALPHA * h + ffn, ln2_g, ln2_b)
    return h, (fk, fv, logf, sk, sv, hr, hi)


def setup_inputs(seed: int = 0) -> dict:
    key = jax.random.key(seed)
    ks = iter(jax.random.split(key, 48))
    f32 = jnp.float32

    def normal(shape, scale=1.0):
        return jax.random.normal(next(ks), shape, f32) * scale

    n_pages = PAST_LEN // PAGE_SIZE
    n_used = DEC_BATCH * n_pages
    n_pool = n_used + n_used // 4

    x_prompt = normal((BATCH, SEQ, D_MODEL))
    x_sample = normal((DEC_BATCH, DEC_SEQ, D_MODEL))
    cache_fox_k = normal((DEPTH, n_pool, PAGE_SIZE, N_FOX_HEADS, HEAD_DIM))
    cache_fox_v = normal((DEPTH, n_pool, PAGE_SIZE, N_FOX_HEADS, HEAD_DIM))
    cache_fox_logf = jax.nn.log_sigmoid(normal((DEPTH, n_pool, PAGE_SIZE, N_FOX_HEADS)) + 2.0)
    cache_sb_k = normal((DEPTH, n_pool, PAGE_SIZE, N_SB_HEADS, HEAD_DIM))
    cache_sb_v = normal((DEPTH, n_pool, PAGE_SIZE, N_SB_HEADS, HEAD_DIM))
    state_ssm_re = normal((DEPTH, DEC_BATCH, N_SSM_GROUPS, SSM_STATE), 0.5)
    state_ssm_im = normal((DEPTH, DEC_BATCH, N_SSM_GROUPS, SSM_STATE), 0.5)
    page_table = jax.random.permutation(next(ks), n_pool)[:n_used].reshape(DEC_BATCH, n_pages).astype(jnp.int32)

    meta_tokens = normal((N_META, D_MODEL))
    ln_in_g = 1.0 + normal((D_MODEL,), 0.02)
    ln_in_b = normal((D_MODEL,), 0.02)
    col_scale = (jnp.ones((IN_W,), f32)
                 .at[OFF_FOX_QKV + 2 * FOX_W:OFF_FOX_F].set(DN_BETA)
                 .at[OFF_SB_QKV + 2 * SB_W:OFF_GATE].set(DN_BETA))
    w_in = normal((DEPTH, D_MODEL, IN_W), D_MODEL ** -0.5) * col_scale
    b_forget = normal((DEPTH, N_FOX_HEADS), 0.1)
    ssm_a_re = -0.5 + normal((DEPTH, N_SSM_GROUPS, SSM_STATE), 0.01)
    ssm_a_im = math.pi * jnp.arange(SSM_STATE, dtype=f32) + normal((DEPTH, N_SSM_GROUPS, SSM_STATE), 0.01)
    ssm_log_dt = jax.random.uniform(next(ks), (DEPTH, N_SSM_GROUPS), f32, math.log(DT_MIN), math.log(DT_MAX))
    ssm_b_re = normal((DEPTH, N_SSM_GROUPS, SSM_STATE, SSM_GROUP), (2 * SSM_GROUP) ** -0.5)
    ssm_b_im = normal((DEPTH, N_SSM_GROUPS, SSM_STATE, SSM_GROUP), (2 * SSM_GROUP) ** -0.5)
    ssm_c_re = normal((DEPTH, N_SSM_GROUPS, SSM_GROUP, SSM_STATE), (2 * SSM_STATE) ** -0.5)
    ssm_c_im = normal((DEPTH, N_SSM_GROUPS, SSM_GROUP, SSM_STATE), (2 * SSM_STATE) ** -0.5)
    ssm_d = normal((DEPTH, N_SSM_GROUPS, SSM_GROUP))
    glu_scale = jnp.concatenate([jnp.full((D_MODEL,), DN_BETA, f32), jnp.ones((D_MODEL,), f32)])
    w_ssm_glu = normal((DEPTH, SSM_W, 2 * D_MODEL), SSM_W ** -0.5) * glu_scale
    w_fox_o = normal((DEPTH, FOX_W, D_MODEL), FOX_W ** -0.5 * DN_BETA)
    w_sb_o = normal((DEPTH, SB_W, D_MODEL), SB_W ** -0.5 * DN_BETA)
    w_out = normal((DEPTH, D_MODEL, D_MODEL), D_MODEL ** -0.5 * DN_BETA)
    ln1_g = 1.0 + normal((DEPTH, D_MODEL), 0.02)
    ln1_b = normal((DEPTH, D_MODEL), 0.02)
    ffn_scale = jnp.concatenate([jnp.ones((D_FF,), f32), jnp.full((D_FF,), DN_BETA, f32)])
    w_ffn_in = normal((DEPTH, D_MODEL, 2 * D_FF), D_MODEL ** -0.5) * ffn_scale
    w_ffn_out = normal((DEPTH, D_FF, D_MODEL), D_FF ** -0.5 * DN_BETA)
    ln2_g = 1.0 + normal((DEPTH, D_MODEL), 0.02)
    ln2_b = normal((DEPTH, D_MODEL), 0.02)
    return {
        "x_prompt": x_prompt, "x_sample": x_sample,
        "cache_fox_k": cache_fox_k, "cache_fox_v": cache_fox_v, "cache_fox_logf": cache_fox_logf,
        "cache_sb_k": cache_sb_k, "cache_sb_v": cache_sb_v,
        "state_ssm_re": state_ssm_re, "state_ssm_im": state_ssm_im,
        "page_table": page_table,
        "meta_tokens": meta_tokens, "ln_in_g": ln_in_g, "ln_in_b": ln_in_b,
        "w_in": w_in, "b_forget": b_forget,
        "ssm_a_re": ssm_a_re, "ssm_a_im": ssm_a_im, "ssm_log_dt": ssm_log_dt,
        "ssm_b_re": ssm_b_re, "ssm_b_im": ssm_b_im, "ssm_c_re": ssm_c_re, "ssm_c_im": ssm_c_im,
        "ssm_d": ssm_d, "w_ssm_glu": w_ssm_glu,
        "w_fox_o": w_fox_o, "w_sb_o": w_sb_o, "w_out": w_out,
        "ln1_g": ln1_g, "ln1_b": ln1_b,
        "w_ffn_in": w_ffn_in, "w_ffn_out": w_ffn_out,
        "ln2_g": ln2_g, "ln2_b": ln2_b,
    }


def reference(x_prompt, x_sample, cache_fox_k, cache_fox_v, cache_fox_logf, cache_sb_k, cache_sb_v,
              state_ssm_re, state_ssm_im, page_table, meta_tokens, ln_in_g, ln_in_b, w_in, b_forget,
              ssm_a_re, ssm_a_im, ssm_log_dt, ssm_b_re, ssm_b_im, ssm_c_re, ssm_c_im, ssm_d, w_ssm_glu,
              w_fox_o, w_sb_o, w_out, ln1_g, ln1_b, w_ffn_in, w_ffn_out, ln2_g, ln2_b):
    bsz = x_prompt.shape[0]
    meta = jnp.broadcast_to(meta_tokens.astype(x_prompt.dtype)[None], (bsz, N_META, D_MODEL))
    hp = layer_norm(jnp.concatenate([meta, x_prompt], axis=1), ln_in_g, ln_in_b)
    hs = layer_norm(x_sample, ln_in_g, ln_in_b)
    zero_state = jnp.zeros((bsz, N_SSM_GROUPS, SSM_STATE), jnp.float32)

    def prompt_fox(q, k, v, logf):
        cum = jnp.cumsum(logf, axis=1)
        return causal_blocks(fox_attend, (q, cum), (k, v, cum))

    def prompt_sb(q, k, v):
        return causal_blocks(sb_attend, (q,), (k, v))

    prompt_rows = [[] for _ in range(7)]
    sample_rows = [[] for _ in range(7)]
    for l in range(DEPTH):
        lp = (w_in[l], b_forget[l], ssm_a_re[l], ssm_a_im[l], ssm_log_dt[l], ssm_b_re[l], ssm_b_im[l],
              ssm_c_re[l], ssm_c_im[l], ssm_d[l], w_ssm_glu[l], w_fox_o[l], w_sb_o[l], w_out[l],
              ln1_g[l], ln1_b[l], w_ffn_in[l], w_ffn_out[l], ln2_g[l], ln2_b[l])

        def sample_fox(q, k, v, logf):
            k_all = jnp.concatenate([paged_past(cache_fox_k, l, page_table), k], axis=1)
            v_all = jnp.concatenate([paged_past(cache_fox_v, l, page_table), v], axis=1)
            lf_all = jnp.concatenate([paged_past(cache_fox_logf, l, page_table).astype(jnp.float32), logf], axis=1)
            cum = jnp.cumsum(lf_all, axis=1)
            past = k_all.shape[1] - q.shape[1]
            k_pos = jnp.arange(k_all.shape[1])
            return fox_attend((q, cum[:, past:]), k_pos[past:], (k_all, v_all, cum), k_pos)

        def sample_sb(q, k, v):
            k_all = jnp.concatenate([paged_past(cache_sb_k, l, page_table), k], axis=1)
            v_all = jnp.concatenate([paged_past(cache_sb_v, l, page_table), v], axis=1)
            past = k_all.shape[1] - q.shape[1]
            k_pos = jnp.arange(k_all.shape[1])
            return sb_attend((q,), k_pos[past:], (k_all, v_all), k_pos)

        hp, rows_p = run_layer(hp, zero_state, zero_state, prompt_fox, prompt_sb, lp)
        hs, rows_s = run_layer(hs, state_ssm_re[l], state_ssm_im[l], sample_fox, sample_sb, lp)
        for acc, r in zip(prompt_rows, rows_p):
            acc.append(r)
        for acc, r in zip(sample_rows, rows_s):
            acc.append(r)

    p_fox_k, p_fox_v, p_fox_logf, p_sb_k, p_sb_v, p_ssm_re, p_ssm_im = [jnp.stack(xs) for xs in prompt_rows]
    s_fox_k, s_fox_v, s_fox_logf, s_sb_k, s_sb_v, s_ssm_re, s_ssm_im = [jnp.stack(xs) for xs in sample_rows]
    y_prompt = hp[:, N_META:]
    y_sample = hs
    return (y_prompt, y_sample,
            p_fox_k, p_fox_v, p_fox_logf, p_sb_k, p_sb_v, p_ssm_re, p_ssm_im,
            s_fox_k, s_fox_v, s_fox_logf, s_sb_k, s_sb_v, s_ssm_re, s_ssm_im)
```

```python
import functools
import math

import jax
import jax.numpy as jnp
from jax import lax
from jax.experimental import pallas as pl
from jax.experimental.pallas import tpu as pltpu

F32 = jnp.float32
BF16 = jnp.bfloat16

D_MODEL = 1024
N_META = 16
HEAD_DIM = 64
N_HEADS = 8
ATT_W = N_HEADS * HEAD_DIM
SSM_W = 512
SSM_GROUP = 16
N_GROUPS = 32
SSM_STATE = 64
STATE_W = N_GROUPS * SSM_STATE
PAGE = 128
LN_EPS = 1e-5
NEG = -1e30

LANES = 128
SUBLANES = 8
VMEM_LIMIT = 48 * 1024 * 1024

_NT = (((1,), (1,)), ((), ()))


def _cparams(n_axes):
    return pltpu.CompilerParams(
        dimension_semantics=("arbitrary",) * n_axes, vmem_limit_bytes=VMEM_LIMIT)


def _sigmoid(x):
    return 1.0 / (1.0 + jnp.exp(-x))


def _log_sigmoid(x):
    return jnp.minimum(x, 0.0) - jnp.log(1.0 + jnp.exp(-jnp.abs(x)))


def _gelu_tanh(x):
    return 0.5 * x * (1.0 + jnp.tanh(math.sqrt(2.0 / math.pi) * (x + 0.044715 * (x * x * x))))


def _layer_norm(x, g, b):
    mu = jnp.mean(x, axis=-1, keepdims=True)
    xc = x - mu
    var = jnp.mean(xc * xc, axis=-1, keepdims=True)
    return xc * lax.rsqrt(var + LN_EPS) * g + b


def _dot01(x, m01):
    hi = x.astype(BF16)
    r1 = x - hi.astype(F32)
    mid = r1.astype(BF16)
    lo = (r1 - mid.astype(F32)).astype(BF16)
    out = jnp.dot(hi, m01, preferred_element_type=F32)
    out = out + jnp.dot(mid, m01, preferred_element_type=F32)
    return out + jnp.dot(lo, m01, preferred_element_type=F32)


def _tri_ones(kind):
    i = jnp.arange(LANES)[:, None]
    j = jnp.arange(LANES)[None, :]
    tri = {"fwd_incl": i <= j, "rev_incl": i >= j, "rev_excl": i > j}[kind]
    return jnp.concatenate([tri.astype(BF16), jnp.ones((LANES, LANES), BF16)], axis=1)


def _ln_kernel(x_ref, g_ref, b_ref, o_ref):
    o_ref[...] = _layer_norm(x_ref[...], g_ref[...], b_ref[...])


def _ln_rows(x, g, b, tm):
    m, d = x.shape
    return pl.pallas_call(
        _ln_kernel,
        grid=(m // tm,),
        in_specs=[pl.BlockSpec((tm, d), lambda i: (i, 0)),
                  pl.BlockSpec((1, d), lambda i: (0, 0)),
                  pl.BlockSpec((1, d), lambda i: (0, 0))],
        out_specs=pl.BlockSpec((tm, d), lambda i: (i, 0)),
        out_shape=jax.ShapeDtypeStruct((m, d), F32),
        compiler_params=_cparams(1),
        name="ln_in",
    )(x, g.reshape(1, d), b.reshape(1, d))


def _proj_kernel(*refs, act, scale, has_bias):
    x_ref, w_ref = refs[0], refs[1]
    pos = 2
    z = jnp.dot(x_ref[...].astype(BF16), w_ref[...], preferred_element_type=F32)
    if has_bias:
        z = z + refs[pos][...]
        pos += 1
    if act == "sigmoid":
        z = _sigmoid(z)
    elif act == "log_sigmoid":
        z = _log_sigmoid(z)
    if scale != 1.0:
        z = z * scale
    for o_ref in refs[pos:]:
        o_ref[...] = z.astype(o_ref.dtype)


def _proj(x, w, out_dtypes, *, tm, tn, act=None, scale=1.0, bias=None, tb_nb=None, name="proj"):
    m, k = x.shape
    n = w.shape[1]
    nj = n // tn
    in_specs = [pl.BlockSpec((tm, k), lambda i, j: (i, 0)),
                pl.BlockSpec((k, tn), lambda i, j: (0, j))]
    args = [x, w]
    if bias is not None:
        in_specs.append(pl.BlockSpec((1, tn), lambda i, j: (0, j)))
        args.append(bias)
    if tb_nb is None:
        out_shape = (m, n)
        out_map = lambda i, j: (i, j)
    else:
        nt = m // tb_nb // tm
        out_shape = (m // tb_nb, tb_nb * n)
        out_map = lambda i, j: (i % nt, (i // nt) * nj + j)
    outs = pl.pallas_call(
        functools.partial(_proj_kernel, act=act, scale=scale, has_bias=bias is not None),
        grid=(m // tm, nj),
        in_specs=in_specs,
        out_specs=[pl.BlockSpec((tm, tn), out_map) for _ in out_dtypes],
        out_shape=[jax.ShapeDtypeStruct(out_shape, dt) for dt in out_dtypes],
        compiler_params=_cparams(2),
        name=name,
    )(*args)
    return outs


def _gated_kernel(x_ref, wa_ref, wb_ref, o_ref, *, silu_a):
    x = x_ref[...].astype(BF16)
    a = jnp.dot(x, wa_ref[...], preferred_element_type=F32)
    b = jnp.dot(x, wb_ref[...], preferred_element_type=F32)
    if silu_a:
        out = a * _sigmoid(a) * b
    else:
        out = a * _sigmoid(b)
    o_ref[...] = out.astype(o_ref.dtype)


def _gated(x, wa, wb, out_dtype, *, tm, tn, silu_a, name):
    m, k = x.shape
    n = wa.shape[1]
    return pl.pallas_call(
        functools.partial(_gated_kernel, silu_a=silu_a),
        grid=(m // tm, n // tn),
        in_specs=[pl.BlockSpec((tm, k), lambda i, j: (i, 0)),
                  pl.BlockSpec((k, tn), lambda i, j: (0, j)),
                  pl.BlockSpec((k, tn), lambda i, j: (0, j))],
        out_specs=pl.BlockSpec((tm, tn), lambda i, j: (i, j)),
        out_shape=jax.ShapeDtypeStruct((m, n), out_dtype),
        compiler_params=_cparams(2),
        name=name,
    )(x, wa, wb)


def _mix_kernel(g_ref, y_ref, af_ref, as_ref, wfo_ref, wso_ref, wout_ref, h_ref, lg_ref, lb_ref,
                o_ref, *, alpha):
    d = D_MODEL
    o_fox = jnp.dot(af_ref[...], wfo_ref[...], preferred_element_type=F32)
    o_sb = jnp.dot(as_ref[...], wso_ref[...], preferred_element_type=F32)
    g = g_ref[...]
    mix = g[:, :d] * y_ref[...] + g[:, d:2 * d] * o_fox + g[:, 2 * d:] * o_sb
    out = jnp.dot(mix.astype(BF16), wout_ref[...], preferred_element_type=F32)
    o_ref[...] = _layer_norm(alpha * h_ref[...] + out, lg_ref[...], lb_ref[...])


def _mix_ln(gates, y_tb, a_fox, a_sb, w_fox_o, w_sb_o, w_out, h, ln_g, ln_b, *, nb, tm, alpha):
    m = h.shape[0]
    n = m // nb
    nt = n // tm
    d = D_MODEL
    bt = lambda w: pl.BlockSpec((tm, w), lambda b, t: (b * nt + t, 0))
    const = lambda r, c: pl.BlockSpec((r, c), lambda b, t: (0, 0))
    return pl.pallas_call(
        functools.partial(_mix_kernel, alpha=alpha),
        grid=(nb, nt),
        in_specs=[bt(3 * d),
                  pl.BlockSpec((tm, d), lambda b, t: (t, b)),
                  bt(ATT_W), bt(ATT_W),
                  const(ATT_W, d), const(ATT_W, d), const(d, d),
                  bt(d), const(1, d), const(1, d)],
        out_specs=bt(d),
        out_shape=jax.ShapeDtypeStruct((m, d), F32),
        compiler_params=_cparams(2),
        name="mix_ln",
    )(gates, y_tb, a_fox, a_sb, w_fox_o, w_sb_o, w_out, h, ln_g.reshape(1, d), ln_b.reshape(1, d))


def _res_ln_kernel(x_ref, w_ref, h_ref, lg_ref, lb_ref, o_ref, *, alpha):
    out = jnp.dot(x_ref[...], w_ref[...], preferred_element_type=F32)
    o_ref[...] = _layer_norm(alpha * h_ref[...] + out, lg_ref[...], lb_ref[...])


def _proj_res_ln(x, w, h, ln_g, ln_b, *, tm, alpha):
    m, k = x.shape
    d = w.shape[1]
    return pl.pallas_call(
        functools.partial(_res_ln_kernel, alpha=alpha),
        grid=(m // tm,),
        in_specs=[pl.BlockSpec((tm, k), lambda i: (i, 0)),
                  pl.BlockSpec((k, d), lambda i: (0, 0)),
                  pl.BlockSpec((tm, d), lambda i: (i, 0)),
                  pl.BlockSpec((1, d), lambda i: (0, 0)),
                  pl.BlockSpec((1, d), lambda i: (0, 0))],
        out_specs=pl.BlockSpec((tm, d), lambda i: (i, 0)),
        out_shape=jax.ShapeDtypeStruct((m, d), F32),
        compiler_params=_cparams(1),
        name="ffn_out_ln",
    )(x, w, h, ln_g.reshape(1, d), ln_b.reshape(1, d))


def _zoh_kernel(ar_ref, ai_ref, ldt_ref, br_ref, bi_ref, abr_ref, abi_ref, bbr_ref, bbi_ref):
    ar, ai = ar_ref[...], ai_ref[...]
    dt = jnp.exp(ldt_ref[...])
    mag = jnp.exp(dt * ar)
    abr = mag * jnp.cos(dt * ai)
    abi = mag * jnp.sin(dt * ai)
    nr, ni = abr - 1.0, abi
    den = ar * ar + ai * ai
    qr = (nr * ar + ni * ai) / den
    qi = (ni * ar - nr * ai) / den
    abr_ref[...] = abr
    abi_ref[...] = abi
    for l in range(ar.shape[0]):
        br, bi = br_ref[l], bi_ref[l]
        bbr_ref[l] = qr[l:l + 1] * br - qi[l:l + 1] * bi
        bbi_ref[l] = qr[l:l + 1] * bi + qi[l:l + 1] * br


def _ssm_kernel(u_ref, h0r_ref, h0i_ref, ar_ref, ai_ref, wb_ref, wc_ref, d_ref,
                y_ref, hr_out_ref, hi_out_ref, bur, bui, sr, si, *, nb, steps):
    @pl.when(pl.program_id(0) == 0)
    def _():
        sr[...] = h0r_ref[...]
        si[...] = h0i_ref[...]

    u = u_ref[...]
    ub = u.astype(BF16)
    n_chunk = SSM_W // LANES
    cw = STATE_W // n_chunk
    for j in range(n_chunk):
        res = jnp.dot(ub[:, j * LANES:(j + 1) * LANES], wb_ref[j], preferred_element_type=F32)
        bur[:, j * cw:(j + 1) * cw] = res[:, :cw]
        bui[:, j * cw:(j + 1) * cw] = res[:, cw:]

    for c in range(n_chunk):
        cs = slice(c * cw, (c + 1) * cw)
        ar = jnp.broadcast_to(ar_ref[:, cs], (nb, cw))
        ai = jnp.broadcast_to(ai_ref[:, cs], (nb, cw))

        def body(t, carry, cs=cs, ar=ar, ai=ai):
            hr, hi = carry
            rows = pl.ds(pl.multiple_of(t * nb, nb), nb)
            nr = ar * hr - ai * hi + bur[rows, cs]
            ni = ar * hi + ai * hr + bui[rows, cs]
            bur[rows, cs] = nr
            bui[rows, cs] = ni
            return nr, ni

        hr, hi = lax.fori_loop(0, steps, body, (sr[:, cs], si[:, cs]))
        sr[:, cs] = hr
        si[:, cs] = hi

    for j in range(n_chunk):
        hrj = bur[:, j * cw:(j + 1) * cw].astype(BF16)
        hij = bui[:, j * cw:(j + 1) * cw].astype(BF16)
        yj = jnp.dot(hrj, wc_ref[j, :cw, :], preferred_element_type=F32)
        yj = yj + jnp.dot(hij, wc_ref[j, cw:, :], preferred_element_type=F32)
        yj = yj + d_ref[:, j * LANES:(j + 1) * LANES] * u[:, j * LANES:(j + 1) * LANES]
        y_ref[:, j * LANES:(j + 1) * LANES] = _gelu_tanh(yj).astype(y_ref.dtype)

    hr_out_ref[...] = sr[...]
    hi_out_ref[...] = si[...]


def _ssm(u_tb, h0r, h0i, abr, abi, wb, wc, dvec, *, nb, steps_per_chunk):
    rows = u_tb.shape[0]
    n_steps = rows // nb
    n_chunks = n_steps // steps_per_chunk
    cr = steps_per_chunk * nb
    const2 = lambda r, c: pl.BlockSpec((r, c), lambda i: (0, 0))
    const3 = lambda a, r, c: pl.BlockSpec((a, r, c), lambda i: (0, 0, 0))
    return pl.pallas_call(
        functools.partial(_ssm_kernel, nb=nb, steps=steps_per_chunk),
        grid=(n_chunks,),
        in_specs=[pl.BlockSpec((cr, SSM_W), lambda i: (i, 0)),
                  const2(nb, STATE_W), const2(nb, STATE_W),
                  const2(1, STATE_W), const2(1, STATE_W),
                  const3(4, LANES, 2 * STATE_W // 4), const3(4, 2 * STATE_W // 4, LANES),
                  const2(1, SSM_W)],
        out_specs=[pl.BlockSpec((cr, SSM_W), lambda i: (i, 0)),
                   const2(nb, STATE_W), const2(nb, STATE_W)],
        out_shape=[jax.ShapeDtypeStruct((rows, SSM_W), BF16),
                   jax.ShapeDtypeStruct((nb, STATE_W), F32),
                   jax.ShapeDtypeStruct((nb, STATE_W), F32)],
        scratch_shapes=[pltpu.VMEM((cr, STATE_W), F32), pltpu.VMEM((cr, STATE_W), F32),
                        pltpu.VMEM((nb, STATE_W), F32), pltpu.VMEM((nb, STATE_W), F32)],
        compiler_params=_cparams(1),
        name="ssm",
    )(u_tb, h0r, h0i, abr, abi, wb, wc, dvec)


def _cumsum_kernel(x_ref, tri_ref, o_ref):
    rows, n = x_ref.shape
    carry = jnp.zeros((rows, LANES), F32)
    for c in range(n // LANES):
        ct = _dot01(x_ref[:, c * LANES:(c + 1) * LANES], tri_ref[...])
        o_ref[:, c * LANES:(c + 1) * LANES] = ct[:, :LANES] + carry
        carry = carry + ct[:, LANES:]


def _cumsum_lanes(x):
    return pl.pallas_call(
        _cumsum_kernel,
        out_shape=jax.ShapeDtypeStruct(x.shape, F32),
        name="cumsum_logf",
    )(x, _tri_ones("fwd_incl"))


def _head_pair_masks():
    lane = lax.broadcasted_iota(jnp.int32, (PAGE, LANES), 1)
    return lane < HEAD_DIM


def _fox_kernel(q_ref, k_ref, v_ref, km_ref, vm_ref, f_ref, o_ref, *, meta_only):
    qi = pl.program_id(2)
    q2 = q_ref[0].astype(F32)
    first_head = _head_pair_masks()
    row = lax.broadcasted_iota(jnp.int32, (PAGE, LANES), 0)
    col = lax.broadcasted_iota(jnp.int32, (PAGE, LANES), 1)
    outs = []
    for hh in range(2):
        qh = jnp.where(first_head if hh == 0 else jnp.logical_not(first_head), q2, 0.0).astype(BF16)
        diag_blk = 0 if meta_only else qi + 1
        fref = f_ref[0, hh, pl.ds(diag_blk, 1), :][:, 0:1]

        def block(fblk, k_blk, v_blk, mask, m, l, acc, qh=qh, fref=fref, hh=hh):
            s = lax.dot_general(qh, k_blk, _NT, preferred_element_type=F32)
            s = s - (f_ref[0, hh, pl.ds(fblk, 1), :] - fref)
            if mask is not None:
                s = jnp.where(mask, s, NEG)
            m_new = jnp.maximum(m, jnp.max(s, axis=1, keepdims=True))
            alpha = jnp.exp(m - m_new)
            p = jnp.exp(s - m_new)
            l = alpha * l + jnp.sum(p, axis=1, keepdims=True)
            acc = alpha * acc + jnp.dot(p.astype(BF16), v_blk, preferred_element_type=F32)
            return m_new, l, acc

        m = jnp.full((PAGE, 1), NEG, F32)
        l = jnp.zeros((PAGE, 1), F32)
        acc = jnp.zeros((PAGE, LANES), F32)
        if meta_only:
            m, l, acc = block(0, km_ref[...], vm_ref[...], (col <= row) & (col < N_META), m, l, acc)
        else:
            m, l, acc = block(0, km_ref[...], vm_ref[...], col < N_META, m, l, acc)

            def body(kb, carry, block=block):
                rows = pl.ds(pl.multiple_of(kb * PAGE, PAGE), PAGE)
                return block(kb + 1, k_ref[0, rows, :], v_ref[0, rows, :], None, *carry)

            m, l, acc = lax.fori_loop(0, qi, body, (m, l, acc))
            rows = pl.ds(pl.multiple_of(qi * PAGE, PAGE), PAGE)
            m, l, acc = block(qi + 1, k_ref[0, rows, :], v_ref[0, rows, :], col <= row, m, l, acc)
        outs.append(acc / l)
    o_ref[0] = jnp.where(first_head, outs[0], outs[1]).astype(o_ref.dtype)


def _sb_kernel(q_ref, k_ref, v_ref, km_ref, vm_ref, tri_ref, o_ref, *, meta_only):
    qi = pl.program_id(2)
    q2 = q_ref[0].astype(F32)
    first_head = _head_pair_masks()
    row = lax.broadcasted_iota(jnp.int32, (PAGE, LANES), 0)
    col = lax.broadcasted_iota(jnp.int32, (PAGE, LANES), 1)
    outs = []
    for hh in range(2):
        qh = jnp.where(first_head if hh == 0 else jnp.logical_not(first_head), q2, 0.0).astype(BF16)

        def block(k_blk, v_blk, mask, r, acc, qh=qh):
            z = lax.dot_general(qh, k_blk, _NT, preferred_element_type=F32)
            lg = _log_sigmoid(-z)
            if mask is not None:
                lg = jnp.where(mask, lg, 0.0)
            ct = _dot01(lg, tri_ref[...])
            w = jnp.exp(z + ct[:, :LANES] + r)
            if mask is not None:
                w = jnp.where(mask, w, 0.0)
            acc = acc + jnp.dot(w.astype(BF16), v_blk, preferred_element_type=F32)
            return r + ct[:, LANES:], acc

        r = jnp.zeros((PAGE, LANES), F32)
        acc = jnp.zeros((PAGE, LANES), F32)
        if meta_only:
            r, acc = block(km_ref[...], vm_ref[...], (col < row) & (col < N_META), r, acc)
        else:
            rows = pl.ds(pl.multiple_of(qi * PAGE, PAGE), PAGE)
            r, acc = block(k_ref[0, rows, :], v_ref[0, rows, :], col < row, r, acc)

            def body(i, carry, block=block):
                kb = qi - 1 - i
                rows = pl.ds(pl.multiple_of(kb * PAGE, PAGE), PAGE)
                return block(k_ref[0, rows, :], v_ref[0, rows, :], None, *carry)

            r, acc = lax.fori_loop(0, qi, body, (r, acc))
            r, acc = block(km_ref[...], vm_ref[...], col < N_META, r, acc)
        outs.append(acc)
    o_ref[0] = jnp.where(first_head, outs[0], outs[1]).astype(o_ref.dtype)


def _prompt_attn(kind, q, kv, kv_meta, extra, *, meta_only=False):
    bsz, n, _ = q.shape
    n_hp = ATT_W // LANES
    kernel = _fox_kernel if kind == "fox" else _sb_kernel
    if kind == "fox":
        extra_spec = pl.BlockSpec((1, 2, extra.shape[2], LANES), lambda b, h, i: (b, h, 0, 0))
    else:
        extra_spec = pl.BlockSpec(extra.shape, lambda b, h, i: (0, 0))
    return pl.pallas_call(
        functools.partial(kernel, meta_only=meta_only),
        grid=(bsz, n_hp, n // PAGE),
        in_specs=[pl.BlockSpec((1, PAGE, LANES), lambda b, h, i: (b, i, h)),
                  pl.BlockSpec((1, n, LANES), lambda b, h, i: (b, 0, h)),
                  pl.BlockSpec((1, n, LANES), lambda b, h, i: (b, 0, n_hp + h)),
                  pl.BlockSpec((PAGE, LANES), lambda b, h, i: (0, h)),
                  pl.BlockSpec((PAGE, LANES), lambda b, h, i: (0, n_hp + h)),
                  extra_spec],
        out_specs=pl.BlockSpec((1, PAGE, LANES), lambda b, h, i: (b, i, h)),
        out_shape=jax.ShapeDtypeStruct((bsz, n, ATT_W), BF16),
        compiler_params=_cparams(3),
        name=kind + ("_meta_attn" if meta_only else "_prompt_attn"),
    )(q, kv, kv, kv_meta, kv_meta, extra)


N_PAGES = 16


def _sample_attn_kernel(pt_ref, qf_ref, qs_ref, kown_ref, vown_ref, lfown_ref, texcl_ref, tincl_ref,
                        *refs):
    del pt_ref
    kf = refs[0 * N_PAGES:1 * N_PAGES]
    vf = refs[1 * N_PAGES:2 * N_PAGES]
    lf = refs[2 * N_PAGES:3 * N_PAGES]
    ks = refs[3 * N_PAGES:4 * N_PAGES]
    vs = refs[4 * N_PAGES:5 * N_PAGES]
    of_ref, os_ref = refs[5 * N_PAGES], refs[5 * N_PAGES + 1]

    row = lax.broadcasted_iota(jnp.int32, (N_HEADS, ATT_W), 0)
    lane = lax.broadcasted_iota(jnp.int32, (N_HEADS, ATT_W), 1)
    own_head = lax.shift_right_logical(lane, 6) == row

    qf32 = jnp.where(own_head, jnp.broadcast_to(qf_ref[0].astype(F32), (N_HEADS, ATT_W)), 0.0)
    qbd = qf32.astype(BF16)
    carry = jnp.broadcast_to(lfown_ref[0], (N_HEADS, LANES))
    scores = [None] * N_PAGES
    for p in reversed(range(N_PAGES)):
        s = lax.dot_general(qbd, kf[p][...].astype(BF16), _NT, preferred_element_type=F32)
        ct = _dot01(lf[p][...], texcl_ref[...])
        scores[p] = s + ct[:, :LANES] + carry
        carry = carry + ct[:, LANES:]
    k_own = kown_ref[0].astype(BF16).astype(F32)
    s_own = jnp.sum(qf32 * k_own, axis=1, keepdims=True)
    m = s_own
    for p in range(N_PAGES):
        m = jnp.maximum(m, jnp.max(scores[p], axis=1, keepdims=True))
    p_own = jnp.exp(s_own - m)
    l = p_own
    acc = p_own * vown_ref[0]
    for p in range(N_PAGES):
        e = jnp.exp(scores[p] - m)
        l = l + jnp.sum(e, axis=1, keepdims=True)
        acc = acc + jnp.dot(e.astype(BF16), vf[p][...].astype(BF16), preferred_element_type=F32)
    acc = jnp.where(own_head, acc / l, 0.0)
    of_ref[0] = jnp.sum(acc, axis=0, keepdims=True).astype(of_ref.dtype)

    qbd = jnp.where(own_head, jnp.broadcast_to(qs_ref[0].astype(F32), (N_HEADS, ATT_W)), 0.0).astype(BF16)
    carry = jnp.zeros((N_HEADS, LANES), F32)
    acc = jnp.zeros((N_HEADS, ATT_W), F32)
    for p in reversed(range(N_PAGES)):
        z = lax.dot_general(qbd, ks[p][...].astype(BF16), _NT, preferred_element_type=F32)
        ct = _dot01(_log_sigmoid(-z), tincl_ref[...])
        w = jnp.exp(z + ct[:, :LANES] + carry)
        carry = carry + ct[:, LANES:]
        acc = acc + jnp.dot(w.astype(BF16), vs[p][...].astype(BF16), preferred_element_type=F32)
    acc = jnp.where(own_head, acc, 0.0)
    os_ref[0] = jnp.sum(acc, axis=0, keepdims=True).astype(os_ref.dtype)


def _sample_attn(layer, page_table, qf, qs, k_own, v_own, lf_own, cfk, cfv, clf_t, csk, csv):
    db = qf.shape[0]
    row3 = lambda w: pl.BlockSpec((1, 1, w), lambda b, pt: (b, 0, 0))
    const = lambda shape: pl.BlockSpec(shape, lambda b, pt: (0, 0))

    def page_specs(rows, width):
        return [pl.BlockSpec((None, None, rows, width),
                             lambda b, pt, p=p: (layer, pt[b * N_PAGES + p], 0, 0))
                for p in range(N_PAGES)]

    in_specs = ([row3(ATT_W), row3(ATT_W), row3(ATT_W), row3(ATT_W),
                 pl.BlockSpec((1, N_HEADS, 1), lambda b, pt: (b, 0, 0)),
                 const((LANES, 2 * LANES)), const((LANES, 2 * LANES))]
                + page_specs(PAGE, ATT_W) + page_specs(PAGE, ATT_W) + page_specs(N_HEADS, PAGE)
                + page_specs(PAGE, ATT_W) + page_specs(PAGE, ATT_W))
    args = ([qf, qs, k_own, v_own, lf_own, _tri_ones("rev_excl"), _tri_ones("rev_incl")]
            + [cfk] * N_PAGES + [cfv] * N_PAGES + [clf_t] * N_PAGES + [csk] * N_PAGES + [csv] * N_PAGES)
    return pl.pallas_call(
        _sample_attn_kernel,
        grid_spec=pltpu.PrefetchScalarGridSpec(
            num_scalar_prefetch=1,
            grid=(db,),
            in_specs=in_specs,
            out_specs=[row3(ATT_W), row3(ATT_W)]),
        out_shape=[jax.ShapeDtypeStruct((db, 1, ATT_W), BF16)] * 2,
        compiler_params=_cparams(1),
        name="sample_attn",
    )(page_table.reshape(-1), *args)


def _block_diag_ssm_weights(bb_re_t, bb_im_t, c_re, c_im):
    eye = jnp.eye(8, dtype=F32)

    def b_blocks(bt):
        b = bt.reshape(SSM_GROUP, 4, 8, SSM_STATE)
        return jnp.einsum("cjgp,gh->jgchp", b, eye).reshape(4, LANES, STATE_W // 4)

    def c_blocks(c):
        c4 = c.reshape(4, 8, SSM_GROUP, SSM_STATE)
        return jnp.einsum("jgcp,gh->jgphc", c4, eye).reshape(4, STATE_W // 4, LANES)

    wb = jnp.concatenate([b_blocks(bb_re_t), b_blocks(bb_im_t)], axis=2).astype(BF16)
    wc = jnp.concatenate([c_blocks(c_re), -c_blocks(c_im)], axis=1).astype(BF16)
    return wb, wc


def kernel(x_prompt, x_sample, cache_fox_k, cache_fox_v, cache_fox_logf, cache_sb_k, cache_sb_v,
           state_ssm_re, state_ssm_im, page_table, meta_tokens, ln_in_g, ln_in_b, w_in, b_forget,
           ssm_a_re, ssm_a_im, ssm_log_dt, ssm_b_re, ssm_b_im, ssm_c_re, ssm_c_im, ssm_d, w_ssm_glu,
           w_fox_o, w_sb_o, w_out, ln1_g, ln1_b, w_ffn_in, w_ffn_out, ln2_g, ln2_b):
    bsz, seq, d = x_prompt.shape
    db = x_sample.shape[0]
    depth = w_in.shape[0]
    d_ff = w_ffn_out.shape[1]
    n_small = db + N_META
    alpha = (2 * depth) ** 0.25
    n_pool = cache_fox_k.shape[1]
    n_blk = seq // PAGE

    off_fq = SSM_W
    off_fkv = off_fq + ATT_W
    off_f = off_fkv + 2 * ATT_W
    off_sq = off_f + N_HEADS
    off_skv = off_sq + ATT_W
    off_g = off_skv + 2 * ATT_W

    flat = lambda a: a.reshape(depth, STATE_W)
    ldt_rep = jnp.repeat(ssm_log_dt, SSM_STATE, axis=1)
    b_t = lambda a: a.reshape(depth, STATE_W, SSM_GROUP).transpose(0, 2, 1)
    abr, abi, bbr_t, bbi_t = pl.pallas_call(
        _zoh_kernel,
        out_shape=[jax.ShapeDtypeStruct((depth, STATE_W), F32)] * 2
        + [jax.ShapeDtypeStruct((depth, SSM_GROUP, STATE_W), F32)] * 2,
        name="zoh",
    )(flat(ssm_a_re), flat(ssm_a_im), ldt_rep, b_t(ssm_b_re), b_t(ssm_b_im))

    w_in16 = w_in.astype(BF16)
    w_glu16 = w_ssm_glu.astype(BF16)
    w_fox_o16 = w_fox_o.astype(BF16)
    w_sb_o16 = w_sb_o.astype(BF16)
    w_out16 = w_out.astype(BF16)
    w_ffn_in16 = w_ffn_in.astype(BF16)
    w_ffn_out16 = w_ffn_out.astype(BF16)

    cfk = cache_fox_k.reshape(depth, n_pool, PAGE, ATT_W)
    cfv = cache_fox_v.reshape(depth, n_pool, PAGE, ATT_W)
    csk = cache_sb_k.reshape(depth, n_pool, PAGE, ATT_W)
    csv = cache_sb_v.reshape(depth, n_pool, PAGE, ATT_W)
    clf_t = cache_fox_logf.astype(F32).transpose(0, 1, 3, 2)

    tri_rev_incl = _tri_ones("rev_incl")

    tm_big = 512
    h_big = _ln_rows(x_prompt.reshape(bsz * seq, d), ln_in_g, ln_in_b, tm_big)
    x_small = jnp.concatenate([x_sample.reshape(db, d), meta_tokens.astype(x_prompt.dtype)], axis=0)
    h_small = _ln_rows(x_small, ln_in_g, ln_in_b, n_small)

    outs = {k: [] for k in ("pfk", "pfv", "plf", "psk", "psv", "phr", "phi",
                            "sfk", "sfv", "slf", "ssk", "ssv", "shr", "shi")}

    for l in range(depth):
        wl = w_in16[l]
        w_f = jnp.pad(wl[:, off_f:off_sq], ((0, 0), (0, LANES - N_HEADS)))
        b_f = jnp.pad(b_forget[l].reshape(1, N_HEADS), ((0, 0), (0, LANES - N_HEADS)))
        wb, wc = _block_diag_ssm_weights(bbr_t[l], bbi_t[l], ssm_c_re[l], ssm_c_im[l])
        dvec = ssm_d[l].reshape(1, SSM_W)
        ar_l, ai_l = abr[l:l + 1], abi[l:l + 1]

        def in_proj(h, tm, tb_nb):
            tn = ATT_W
            (u,) = _proj(h, wl[:, :off_fq], [F32], tm=tm, tn=tn, tb_nb=tb_nb, name="in_u")
            (fq,) = _proj(h, wl[:, off_fq:off_fkv], [BF16], tm=tm, tn=tn, scale=HEAD_DIM ** -0.5,
                          name="in_fox_q")
            fkv32, fkv16 = _proj(h, wl[:, off_fkv:off_f], [F32, BF16], tm=tm, tn=tn, name="in_fox_kv")
            (lf,) = _proj(h, w_f, [F32], tm=tm, tn=LANES, bias=b_f, act="log_sigmoid", name="in_logf")
            (sq,) = _proj(h, wl[:, off_sq:off_skv], [BF16], tm=tm, tn=tn, scale=HEAD_DIM ** -0.5,
                          name="in_sb_q")
            skv32, skv16 = _proj(h, wl[:, off_skv:off_g], [F32, BF16], tm=tm, tn=tn, name="in_sb_kv")
            (gates,) = _proj(h, wl[:, off_g:], [F32], tm=tm, tn=2 * tn, act="sigmoid", name="in_gates")
            return u, fq, fkv32, fkv16, lf[:, :N_HEADS], sq, skv32, skv16, gates

        u_s, fq_s, fkv32_s, fkv16_s, lf_s, sq_s, skv32_s, skv16_s, gates_s = in_proj(h_small, n_small, None)

        y_smp, shr, shi = _ssm(u_s[:db], state_ssm_re[l].reshape(db, STATE_W),
                               state_ssm_im[l].reshape(db, STATE_W), ar_l, ai_l, wb, wc, dvec,
                               nb=db, steps_per_chunk=1)
        zeros8 = jnp.zeros((SUBLANES, STATE_W), F32)
        y_meta, mhr, mhi = _ssm(jnp.repeat(u_s[db:], SUBLANES, axis=0), zeros8, zeros8,
                                ar_l, ai_l, wb, wc, dvec, nb=SUBLANES, steps_per_chunk=N_META)
        y_small = jnp.concatenate([y_smp, y_meta[::SUBLANES]], axis=0)

        u_b, fq_b, fkv32_b, fkv16_b, lf_b, sq_b, skv32_b, skv16_b, gates_b = in_proj(h_big, tm_big, bsz)

        lf_meta_t = jnp.pad(lf_s[db:].T, ((0, 0), (0, LANES - N_META)))
        lf_real_t = lf_b.reshape(bsz, seq, N_HEADS).transpose(0, 2, 1)
        lf_all = jnp.concatenate(
            [jnp.broadcast_to(lf_meta_t[None], (bsz, N_HEADS, LANES)), lf_real_t], axis=2)
        f_all = _cumsum_lanes(lf_all.reshape(bsz * N_HEADS, LANES + seq))
        f_all = f_all.reshape(bsz, N_HEADS, n_blk + 1, LANES)

        pad_meta = lambda a: jnp.pad(a[db:], ((0, PAGE - N_META), (0, 0)))
        fkv_meta = pad_meta(fkv16_s)
        skv_meta = pad_meta(skv16_s)

        a_fox_meta = _prompt_attn("fox", pad_meta(fq_s)[None], fkv_meta[None], fkv_meta,
                                  f_all[:1], meta_only=True)[0, :N_META]
        a_sb_meta = _prompt_attn("sb", pad_meta(sq_s)[None], skv_meta[None], skv_meta,
                                 tri_rev_incl, meta_only=True)[0, :N_META]

        a_fox_smp, a_sb_smp = _sample_attn(
            l, page_table, fq_s[:db].reshape(db, 1, ATT_W), sq_s[:db].reshape(db, 1, ATT_W),
            fkv32_s[:db, :ATT_W].reshape(db, 1, ATT_W), fkv32_s[:db, ATT_W:].reshape(db, 1, ATT_W),
            lf_s[:db].reshape(db, N_HEADS, 1), cfk, cfv, clf_t, csk, csv)
        a_fox_small = jnp.concatenate([a_fox_smp.reshape(db, ATT_W), a_fox_meta], axis=0)
        a_sb_small = jnp.concatenate([a_sb_smp.reshape(db, ATT_W), a_sb_meta], axis=0)

        u_b = u_b.reshape(seq * bsz, SSM_W)
        y_big, phr, phi = _ssm(u_b, mhr, mhi, ar_l, ai_l, wb, wc, dvec, nb=bsz, steps_per_chunk=64)
        a_fox_big = _prompt_attn("fox", fq_b.reshape(bsz, seq, ATT_W),
                                 fkv16_b.reshape(bsz, seq, 2 * ATT_W), fkv_meta, f_all)
        a_sb_big = _prompt_attn("sb", sq_b.reshape(bsz, seq, ATT_W),
                                skv16_b.reshape(bsz, seq, 2 * ATT_W), skv_meta, tri_rev_incl)

        def tail(h, y, a_fox, a_sb, gates, nb, tm):
            m = h.shape[0]
            glu = _gated(y, w_glu16[l][:, :d], w_glu16[l][:, d:], F32, tm=tm, tn=d, silu_a=False,
                         name="ssm_glu")
            h1 = _mix_ln(gates, glu.reshape(m // nb, nb * d), a_fox, a_sb, w_fox_o16[l], w_sb_o16[l],
                         w_out16[l], h, ln1_g[l], ln1_b[l], nb=nb, tm=tm, alpha=alpha)
            act = _gated(h1, w_ffn_in16[l][:, :d_ff], w_ffn_in16[l][:, d_ff:], BF16, tm=tm,
                         tn=d_ff // 2, silu_a=True, name="ffn_in")
            return _proj_res_ln(act, w_ffn_out16[l], h1, ln2_g[l], ln2_b[l], tm=tm, alpha=alpha)

        h_small = tail(h_small, y_small, a_fox_small, a_sb_small, gates_s, 1, n_small)
        h_big = tail(h_big, y_big, a_fox_big.reshape(bsz * seq, ATT_W),
                     a_sb_big.reshape(bsz * seq, ATT_W), gates_b, bsz, tm_big)

        def with_meta(real, meta):
            w = real.shape[1]
            return jnp.concatenate(
                [jnp.broadcast_to(meta[None], (bsz, N_META, w)), real.reshape(bsz, seq, w)], axis=1)

        heads = lambda a: a.reshape(a.shape[:-1] + (N_HEADS, HEAD_DIM))
        outs["pfk"].append(heads(with_meta(fkv32_b[:, :ATT_W], fkv32_s[db:, :ATT_W])))
        outs["pfv"].append(heads(with_meta(fkv32_b[:, ATT_W:], fkv32_s[db:, ATT_W:])))
        outs["plf"].append(with_meta(lf_b, lf_s[db:]))
        outs["psk"].append(heads(with_meta(skv32_b[:, :ATT_W], skv32_s[db:, :ATT_W])))
        outs["psv"].append(heads(with_meta(skv32_b[:, ATT_W:], skv32_s[db:, ATT_W:])))
        outs["phr"].append(phr.reshape(bsz, N_GROUPS, SSM_STATE))
        outs["phi"].append(phi.reshape(bsz, N_GROUPS, SSM_STATE))
        outs["sfk"].append(heads(fkv32_s[:db, :ATT_W].reshape(db, 1, ATT_W)))
        outs["sfv"].append(heads(fkv32_s[:db, ATT_W:].reshape(db, 1, ATT_W)))
        outs["slf"].append(lf_s[:db].reshape(db, 1, N_HEADS))
        outs["ssk"].append(heads(skv32_s[:db, :ATT_W].reshape(db, 1, ATT_W)))
        outs["ssv"].append(heads(skv32_s[:db, ATT_W:].reshape(db, 1, ATT_W)))
        outs["shr"].append(shr.reshape(db, N_GROUPS, SSM_STATE))
        outs["shi"].append(shi.reshape(db, N_GROUPS, SSM_STATE))

    st = {k: jnp.stack(v) for k, v in outs.items()}
    return (h_big.reshape(bsz, seq, d), h_small[:db].reshape(db, 1, d),
            st["pfk"], st["pfv"], st["plf"], st["psk"], st["psv"], st["phr"], st["phi"],
            st["sfk"], st["sfv"], st["slf"], st["ssk"], st["ssv"], st["shr"], st["shi"])
```

```python
import functools
import math

import jax
import jax.numpy as jnp
from jax import lax
from jax.experimental import pallas as pl
from jax.experimental.pallas import tpu as pltpu

F32 = jnp.float32
BF16 = jnp.bfloat16

D_MODEL = 1024
N_META = 16
HEAD_DIM = 64
N_HEADS = 8
ATT_W = N_HEADS * HEAD_DIM
SSM_W = 512
SSM_GROUP = 16
N_GROUPS = 32
SSM_STATE = 64
STATE_W = N_GROUPS * SSM_STATE
PAGE = 128
N_PAGES = 16
PAGE_ELEMS = PAGE * N_HEADS
ATT_BLK = 256
LN_EPS = 1e-5
NEG = -1e30
LOG2E = math.log2(math.e)
Q_SCALE = LOG2E * HEAD_DIM ** -0.5

LANES = 128
SUBLANES = 8
VMEM_LIMIT = 48 * 1024 * 1024

_NT = (((1,), (1,)), ((), ()))


def _cparams(n_axes):
    return pltpu.CompilerParams(
        dimension_semantics=("arbitrary",) * n_axes, vmem_limit_bytes=VMEM_LIMIT)


def _sigmoid(x):
    return 1.0 / (1.0 + jnp.exp(-x))


def _log_sigmoid(x):
    return jnp.minimum(x, 0.0) - jnp.log(1.0 + jnp.exp(-jnp.abs(x)))


def _log2_sigmoid_neg(z2):
    nz = -z2
    return jnp.minimum(nz, 0.0) - jnp.log2(1.0 + jnp.exp2(jnp.minimum(z2, nz)))


def _gelu_tanh(x):
    return 0.5 * x * (1.0 + jnp.tanh(math.sqrt(2.0 / math.pi) * (x + 0.044715 * (x * x * x))))


def _layer_norm(x, g, b):
    mu = jnp.mean(x, axis=-1, keepdims=True)
    xc = x - mu
    var = jnp.mean(xc * xc, axis=-1, keepdims=True)
    return xc * lax.rsqrt(var + LN_EPS) * g + b


def _split_bf16(x, n_terms):
    terms = []
    for _ in range(n_terms - 1):
        t = x.astype(BF16)
        terms.append(t)
        x = x - t.astype(F32)
    terms.append(x.astype(BF16))
    return terms


def _dot01(x, m01, n_terms=3):
    rows = x.shape[0]
    out = jnp.dot(jnp.concatenate(_split_bf16(x, n_terms), axis=0), m01, preferred_element_type=F32)
    acc = out[:rows]
    for t in range(1, n_terms):
        acc = acc + out[t * rows:(t + 1) * rows]
    return acc


def _tri_ones(kind, n=LANES):
    i = jnp.arange(n)[:, None]
    j = jnp.arange(n)[None, :]
    tri = {"fwd_incl": i <= j, "rev_incl": i >= j}[kind]
    return jnp.concatenate([tri.astype(BF16), jnp.ones((n, LANES), BF16)], axis=1)


def _interleaved_cumsum_mat(inclusive):
    r = jnp.arange(PAGE_ELEMS)
    i, hp = (r // N_HEADS)[:, None], (r % N_HEADS)[:, None]
    j, h = (r // N_HEADS)[None, :], (r % N_HEADS)[None, :]
    later = (i >= j) if inclusive else (i > j)
    tot = hp == (jnp.arange(LANES) % N_HEADS)[None, :]
    return jnp.concatenate([(hp == h) & later, tot], axis=1).astype(BF16)


def _ln_kernel(x_ref, g_ref, b_ref, o_ref):
    o_ref[...] = _layer_norm(x_ref[...], g_ref[...], b_ref[...])


def _ln_rows(x, g, b, tm):
    m, d = x.shape
    return pl.pallas_call(
        _ln_kernel,
        grid=(m // tm,),
        in_specs=[pl.BlockSpec((tm, d), lambda i: (i, 0)),
                  pl.BlockSpec((1, d), lambda i: (0, 0)),
                  pl.BlockSpec((1, d), lambda i: (0, 0))],
        out_specs=pl.BlockSpec((tm, d), lambda i: (i, 0)),
        out_shape=jax.ShapeDtypeStruct((m, d), F32),
        compiler_params=_cparams(1),
        name="ln_in",
    )(x, g.reshape(1, d), b.reshape(1, d))


def _proj_kernel(*refs, act, scale, has_bias):
    x_ref, w_ref = refs[0], refs[1]
    pos = 2
    z = jnp.dot(x_ref[...].astype(BF16), w_ref[...], preferred_element_type=F32)
    if has_bias:
        z = z + refs[pos][...]
        pos += 1
    if act == "sigmoid":
        z = _sigmoid(z)
    elif act == "log_sigmoid":
        z = _log_sigmoid(z)
    if scale != 1.0:
        z = z * scale
    for o_ref in refs[pos:]:
        o_ref[...] = z.astype(o_ref.dtype)


def _proj(x, w, out_dtypes, *, tm, tn, act=None, scale=1.0, bias=None, tb_nb=None, name="proj"):
    m, k = x.shape
    n = w.shape[1]
    nj = n // tn
    in_specs = [pl.BlockSpec((tm, k), lambda i, j: (i, 0)),
                pl.BlockSpec((k, tn), lambda i, j: (0, j))]
    args = [x, w]
    if bias is not None:
        in_specs.append(pl.BlockSpec((1, tn), lambda i, j: (0, j)))
        args.append(bias)
    if tb_nb is None:
        out_shape = (m, n)
        out_map = lambda i, j: (i, j)
    else:
        nt = m // tb_nb // tm
        out_shape = (m // tb_nb, tb_nb * n)
        out_map = lambda i, j: (i % nt, (i // nt) * nj + j)
    outs = pl.pallas_call(
        functools.partial(_proj_kernel, act=act, scale=scale, has_bias=bias is not None),
        grid=(m // tm, nj),
        in_specs=in_specs,
        out_specs=[pl.BlockSpec((tm, tn), out_map) for _ in out_dtypes],
        out_shape=[jax.ShapeDtypeStruct(out_shape, dt) for dt in out_dtypes],
        compiler_params=_cparams(2),
        name=name,
    )(*args)
    return outs


def _gated_kernel(x_ref, wa_ref, wb_ref, o_ref, *, silu_a):
    x = x_ref[...].astype(BF16)
    a = jnp.dot(x, wa_ref[...], preferred_element_type=F32)
    b = jnp.dot(x, wb_ref[...], preferred_element_type=F32)
    if silu_a:
        out = a * _sigmoid(a) * b
    else:
        out = a * _sigmoid(b)
    o_ref[...] = out.astype(o_ref.dtype)


def _gated(x, wa, wb, out_dtype, *, tm, tn, silu_a, name):
    m, k = x.shape
    n = wa.shape[1]
    return pl.pallas_call(
        functools.partial(_gated_kernel, silu_a=silu_a),
        grid=(m // tm, n // tn),
        in_specs=[pl.BlockSpec((tm, k), lambda i, j: (i, 0)),
                  pl.BlockSpec((k, tn), lambda i, j: (0, j)),
                  pl.BlockSpec((k, tn), lambda i, j: (0, j))],
        out_specs=pl.BlockSpec((tm, tn), lambda i, j: (i, j)),
        out_shape=jax.ShapeDtypeStruct((m, n), out_dtype),
        compiler_params=_cparams(2),
        name=name,
    )(x, wa, wb)


def _mix_kernel(g_ref, y_ref, af_ref, as_ref, wfo_ref, wso_ref, wout_ref, h_ref, lg_ref, lb_ref,
                o_ref, *, alpha):
    d = D_MODEL
    o_fox = jnp.dot(af_ref[...], wfo_ref[...], preferred_element_type=F32)
    o_sb = jnp.dot(as_ref[...], wso_ref[...], preferred_element_type=F32)
    g = g_ref[...]
    mix = g[:, :d] * y_ref[...] + g[:, d:2 * d] * o_fox + g[:, 2 * d:] * o_sb
    out = jnp.dot(mix.astype(BF16), wout_ref[...], preferred_element_type=F32)
    o_ref[...] = _layer_norm(alpha * h_ref[...] + out, lg_ref[...], lb_ref[...])


def _mix_ln(gates, y_tb, a_fox, a_sb, w_fox_o, w_sb_o, w_out, h, ln_g, ln_b, *, nb, tm, alpha):
    m = h.shape[0]
    n = m // nb
    nt = n // tm
    d = D_MODEL
    bt = lambda w: pl.BlockSpec((tm, w), lambda b, t: (b * nt + t, 0))
    const = lambda r, c: pl.BlockSpec((r, c), lambda b, t: (0, 0))
    return pl.pallas_call(
        functools.partial(_mix_kernel, alpha=alpha),
        grid=(nb, nt),
        in_specs=[bt(3 * d),
                  pl.BlockSpec((tm, d), lambda b, t: (t, b)),
                  bt(ATT_W), bt(ATT_W),
                  const(ATT_W, d), const(ATT_W, d), const(d, d),
                  bt(d), const(1, d), const(1, d)],
        out_specs=bt(d),
        out_shape=jax.ShapeDtypeStruct((m, d), F32),
        compiler_params=_cparams(2),
        name="mix_ln",
    )(gates, y_tb, a_fox, a_sb, w_fox_o, w_sb_o, w_out, h, ln_g.reshape(1, d), ln_b.reshape(1, d))


def _res_ln_kernel(x_ref, w_ref, h_ref, lg_ref, lb_ref, o_ref, *, alpha):
    out = jnp.dot(x_ref[...], w_ref[...], preferred_element_type=F32)
    o_ref[...] = _layer_norm(alpha * h_ref[...] + out, lg_ref[...], lb_ref[...])


def _proj_res_ln(x, w, h, ln_g, ln_b, *, tm, alpha):
    m, k = x.shape
    d = w.shape[1]
    return pl.pallas_call(
        functools.partial(_res_ln_kernel, alpha=alpha),
        grid=(m // tm,),
        in_specs=[pl.BlockSpec((tm, k), lambda i: (i, 0)),
                  pl.BlockSpec((k, d), lambda i: (0, 0)),
                  pl.BlockSpec((tm, d), lambda i: (i, 0)),
                  pl.BlockSpec((1, d), lambda i: (0, 0)),
                  pl.BlockSpec((1, d), lambda i: (0, 0))],
        out_specs=pl.BlockSpec((tm, d), lambda i: (i, 0)),
        out_shape=jax.ShapeDtypeStruct((m, d), F32),
        compiler_params=_cparams(1),
        name="ffn_out_ln",
    )(x, w, h, ln_g.reshape(1, d), ln_b.reshape(1, d))


def _zoh_kernel(ar_ref, ai_ref, ldt_ref, br_ref, bi_ref, abr_ref, abi_ref, bbr_ref, bbi_ref):
    ar, ai = ar_ref[...], ai_ref[...]
    dt = jnp.exp(ldt_ref[...])
    mag = jnp.exp(dt * ar)
    abr = mag * jnp.cos(dt * ai)
    abi = mag * jnp.sin(dt * ai)
    nr, ni = abr - 1.0, abi
    den = ar * ar + ai * ai
    qr = (nr * ar + ni * ai) / den
    qi = (ni * ar - nr * ai) / den
    abr_ref[...] = abr
    abi_ref[...] = abi
    for l in range(ar.shape[0]):
        br, bi = br_ref[l], bi_ref[l]
        bbr_ref[l] = qr[l:l + 1] * br - qi[l:l + 1] * bi
        bbi_ref[l] = qr[l:l + 1] * bi + qi[l:l + 1] * br


def _ssm_kernel(u_ref, h0r_ref, h0i_ref, ar_ref, ai_ref, wb_ref, wc_ref, d_ref,
                y_ref, hr_out_ref, hi_out_ref, bur, bui, sr, si, *, nb, steps):
    @pl.when(pl.program_id(0) == 0)
    def _():
        sr[...] = h0r_ref[...]
        si[...] = h0i_ref[...]

    u = u_ref[...]
    ub = u.astype(BF16)
    n_chunk = SSM_W // LANES
    cw = STATE_W // n_chunk
    for j in range(n_chunk):
        res = jnp.dot(ub[:, j * LANES:(j + 1) * LANES], wb_ref[j], preferred_element_type=F32)
        bur[:, j * cw:(j + 1) * cw] = res[:, :cw]
        bui[:, j * cw:(j + 1) * cw] = res[:, cw:]

    for c in range(n_chunk):
        cs = slice(c * cw, (c + 1) * cw)
        ar = jnp.broadcast_to(ar_ref[:, cs], (nb, cw))
        ai = jnp.broadcast_to(ai_ref[:, cs], (nb, cw))

        def body(t, carry, cs=cs, ar=ar, ai=ai):
            hr, hi = carry
            rows = pl.ds(pl.multiple_of(t * nb, nb), nb)
            nr = ar * hr - ai * hi + bur[rows, cs]
            ni = ar * hi + ai * hr + bui[rows, cs]
            bur[rows, cs] = nr
            bui[rows, cs] = ni
            return nr, ni

        hr, hi = lax.fori_loop(0, steps, body, (sr[:, cs], si[:, cs]))
        sr[:, cs] = hr
        si[:, cs] = hi

    for j in range(n_chunk):
        hrj = bur[:, j * cw:(j + 1) * cw].astype(BF16)
        hij = bui[:, j * cw:(j + 1) * cw].astype(BF16)
        yj = jnp.dot(hrj, wc_ref[j, :cw, :], preferred_element_type=F32)
        yj = yj + jnp.dot(hij, wc_ref[j, cw:, :], preferred_element_type=F32)
        yj = yj + d_ref[:, j * LANES:(j + 1) * LANES] * u[:, j * LANES:(j + 1) * LANES]
        y_ref[:, j * LANES:(j + 1) * LANES] = _gelu_tanh(yj).astype(y_ref.dtype)

    hr_out_ref[...] = sr[...]
    hi_out_ref[...] = si[...]


def _ssm(u_tb, h0r, h0i, abr, abi, wb, wc, dvec, *, nb, steps_per_chunk):
    rows = u_tb.shape[0]
    n_steps = rows // nb
    n_chunks = n_steps // steps_per_chunk
    cr = steps_per_chunk * nb
    const2 = lambda r, c: pl.BlockSpec((r, c), lambda i: (0, 0))
    const3 = lambda a, r, c: pl.BlockSpec((a, r, c), lambda i: (0, 0, 0))
    return pl.pallas_call(
        functools.partial(_ssm_kernel, nb=nb, steps=steps_per_chunk),
        grid=(n_chunks,),
        in_specs=[pl.BlockSpec((cr, SSM_W), lambda i: (i, 0)),
                  const2(nb, STATE_W), const2(nb, STATE_W),
                  const2(1, STATE_W), const2(1, STATE_W),
                  const3(4, LANES, 2 * STATE_W // 4), const3(4, 2 * STATE_W // 4, LANES),
                  const2(1, SSM_W)],
        out_specs=[pl.BlockSpec((cr, SSM_W), lambda i: (i, 0)),
                   const2(nb, STATE_W), const2(nb, STATE_W)],
        out_shape=[jax.ShapeDtypeStruct((rows, SSM_W), BF16),
                   jax.ShapeDtypeStruct((nb, STATE_W), F32),
                   jax.ShapeDtypeStruct((nb, STATE_W), F32)],
        scratch_shapes=[pltpu.VMEM((cr, STATE_W), F32), pltpu.VMEM((cr, STATE_W), F32),
                        pltpu.VMEM((nb, STATE_W), F32), pltpu.VMEM((nb, STATE_W), F32)],
        compiler_params=_cparams(1),
        name="ssm",
    )(u_tb, h0r, h0i, abr, abi, wb, wc, dvec)


def _cumsum_kernel(x_ref, tri_ref, o_ref):
    rows, n = x_ref.shape
    carry = jnp.zeros((rows, LANES), F32)
    for c in range(n // LANES):
        ct = _dot01(x_ref[:, c * LANES:(c + 1) * LANES], tri_ref[...])
        o_ref[:, c * LANES:(c + 1) * LANES] = ct[:, :LANES] + carry
        carry = carry + ct[:, LANES:]


def _cumsum_lanes(x):
    return pl.pallas_call(
        _cumsum_kernel,
        out_shape=jax.ShapeDtypeStruct(x.shape, F32),
        name="cumsum_logf",
    )(x, _tri_ones("fwd_incl"))


def _attn_prologue(q_ref, v_ref, kvm_ref, qm_s, va_s, vb_s, vma_s, vmb_s, *, tq, n_real, ones_cols):
    lane512 = lax.broadcasted_iota(jnp.int32, (1, ATT_W), 1)
    even = ((lane512 >> 6) & 1) == 0

    def fill(dst_a, dst_b, rows, v):
        va = jnp.where(even, v, 0.0).astype(BF16)
        vb = jnp.where(even, 0.0, v).astype(BF16)
        if not ones_cols:
            dst_a[rows, :] = va
            dst_b[rows, :] = vb
            return
        ones = jnp.ones((v.shape[0], LANES), BF16)
        for p in range(N_HEADS // 2):
            src = slice(p * LANES, (p + 1) * LANES)
            for dst, val in ((dst_a, va), (dst_b, vb)):
                dst[rows, 2 * p * LANES:(2 * p + 1) * LANES] = val[:, src]
                dst[rows, (2 * p + 1) * LANES:(2 * p + 2) * LANES] = ones

    @pl.when(pl.program_id(1) == 0)
    def _():
        fill(vma_s, vmb_s, slice(0, PAGE), kvm_ref[:, ATT_W:].astype(F32))
        for r in range(n_real // ATT_BLK):
            rows = slice(r * ATT_BLK, (r + 1) * ATT_BLK)
            fill(va_s, vb_s, rows, v_ref[0, rows, :].astype(F32))

    first_head = lax.broadcasted_iota(jnp.int32, (tq, LANES), 1) < HEAD_DIM
    q = q_ref[0].astype(F32)
    for h in range(N_HEADS):
        qp = q[:, (h // 2) * LANES:(h // 2 + 1) * LANES]
        keep = first_head if h % 2 == 0 else jnp.logical_not(first_head)
        qm_s[h] = jnp.where(keep, qp, 0.0).astype(BF16)
    return first_head


def _fox_kernel(q_ref, k_ref, v_ref, kvm_ref, f_ref, o_ref,
                qm_s, va_s, vb_s, vma_s, vmb_s, m_s, l_s, acc_s, a_s, s_scr, p_scr,
                *, meta_only, tq, n_real):
    j = pl.program_id(1)
    first_head = _attn_prologue(q_ref, v_ref, kvm_ref, qm_s, va_s, vb_s, vma_s, vmb_s,
                                tq=tq, n_real=n_real, ones_cols=True)
    m_s[...] = jnp.full(m_s.shape, NEG, F32)
    l_s[...] = jnp.zeros(l_s.shape, F32)
    acc_s[...] = jnp.zeros(acc_s.shape, F32)
    diag_idx = 0 if meta_only else j + 1
    fref = [f_ref[0, h, pl.ds(diag_idx, 1), :][:, 0:1] for h in range(N_HEADS)]

    def block(kget, vaget, vbget, fidx, mask, tk):
        for p in range(N_HEADS // 2):
            k2 = kget(slice(p * LANES, (p + 1) * LANES))
            s2 = lax.dot_general(qm_s[2 * p:2 * p + 2].reshape(2 * tq, LANES), k2, _NT,
                                 preferred_element_type=F32)
            s_scr[2 * p, :, :tk] = s2[:tq]
            s_scr[2 * p + 1, :, :tk] = s2[tq:]

        def scores(h):
            frow = f_ref[0, h, pl.ds(fidx, 1), :][:, :tk]
            s = s_scr[h, :, :tk] - (frow - fref[h]) * LOG2E
            return s if mask is None else jnp.where(mask, s, NEG)

        for h in range(N_HEADS):
            m_old = m_s[h]
            m_new = jnp.maximum(m_old, jnp.max(scores(h), axis=1, keepdims=True))
            a_s[h] = jnp.exp2(m_old - m_new)
            m_s[h] = m_new
        for h in range(N_HEADS):
            m = m_s[h]
            m = m if tk == LANES else jnp.concatenate([m] * (tk // LANES), axis=1)
            p_scr[h, :, :tk] = jnp.exp2(scores(h) - m).astype(BF16)
        for p in range(N_HEADS // 2):
            aug = slice(2 * p * LANES, (2 * p + 2) * LANES)
            da = jnp.dot(p_scr[2 * p, :, :tk], vaget(aug), preferred_element_type=F32)
            db = jnp.dot(p_scr[2 * p + 1, :, :tk], vbget(aug), preferred_element_type=F32)
            l_s[2 * p] = a_s[2 * p] * l_s[2 * p] + da[:, LANES:]
            l_s[2 * p + 1] = a_s[2 * p + 1] * l_s[2 * p + 1] + db[:, LANES:]
            acc_s[p] = (jnp.where(first_head, a_s[2 * p], a_s[2 * p + 1]) * acc_s[p]
                        + da[:, :LANES] + db[:, :LANES])

    def meta_block(mask):
        block(lambda ln: kvm_ref[:, ln], lambda ln: vma_s[:, ln], lambda ln: vmb_s[:, ln],
              0, mask, PAGE)

    col_m = lax.broadcasted_iota(jnp.int32, (tq, PAGE), 1)
    if meta_only:
        row_m = lax.broadcasted_iota(jnp.int32, (tq, PAGE), 0)
        meta_block((col_m <= row_m) & (col_m < N_META))
    else:
        meta_block(col_m < N_META)

        def chunk(c, mask):
            rows = pl.ds(pl.multiple_of(c * ATT_BLK, ATT_BLK), ATT_BLK)
            block(lambda ln: k_ref[0, rows, ln], lambda ln: va_s[rows, ln],
                  lambda ln: vb_s[rows, ln], c + 1, mask, ATT_BLK)

        def body(c, carry):
            chunk(c, None)
            return carry

        lax.fori_loop(0, j, body, 0)
        row = lax.broadcasted_iota(jnp.int32, (tq, ATT_BLK), 0)
        col = lax.broadcasted_iota(jnp.int32, (tq, ATT_BLK), 1)
        chunk(j, col <= row)

    for p in range(N_HEADS // 2):
        l2 = jnp.where(first_head, l_s[2 * p], l_s[2 * p + 1])
        o_ref[0, :, p * LANES:(p + 1) * LANES] = (acc_s[p] / l2).astype(o_ref.dtype)


def _sb_kernel(q_ref, k_ref, v_ref, kvm_ref, tri_ref, trim_ref, o_ref,
               qm_s, va_s, vb_s, vma_s, vmb_s, r_s, acc_s, z_scr, hl_scr, p_scr,
               *, meta_only, tq, n_real):
    j = pl.program_id(1)
    _attn_prologue(q_ref, v_ref, kvm_ref, qm_s, va_s, vb_s, vma_s, vmb_s, tq=tq, n_real=n_real,
                   ones_cols=False)
    r_s[...] = jnp.zeros(r_s.shape, F32)
    acc_s[...] = jnp.zeros(acc_s.shape, F32)

    def block(kget, vaget, vbget, tri, mask, tk):
        for p in range(N_HEADS // 2):
            k2 = kget(slice(p * LANES, (p + 1) * LANES))
            z2 = lax.dot_general(qm_s[2 * p:2 * p + 2].reshape(2 * tq, LANES), k2, _NT,
                                 preferred_element_type=F32)
            z_scr[2 * p, :, :tk] = z2[:tq]
            z_scr[2 * p + 1, :, :tk] = z2[tq:]
        for h in range(N_HEADS):
            lg = _log2_sigmoid_neg(z_scr[h, :, :tk])
            if mask is not None:
                lg = jnp.where(mask, lg, 0.0)
            hi = lg.astype(BF16)
            hl_scr[h, :tq, :tk] = hi
            hl_scr[h, tq:, :tk] = (lg - hi.astype(F32)).astype(BF16)
        for h in range(N_HEADS):
            out = jnp.dot(hl_scr[h, :, :tk], tri, preferred_element_type=F32)
            ct = out[:tq] + out[tq:]
            r = r_s[h]
            later = r if tk == LANES else jnp.concatenate([r] * (tk // LANES), axis=1)
            w = jnp.exp2(z_scr[h, :, :tk] + ct + later)
            if mask is not None:
                w = jnp.where(mask, w, 0.0)
            p_scr[h, :, :tk] = w.astype(BF16)
            r_s[h] = r + jnp.broadcast_to(ct[:, 0:1], (tq, LANES))
        for p in range(N_HEADS // 2):
            lanes = slice(p * LANES, (p + 1) * LANES)
            da = jnp.dot(p_scr[2 * p, :, :tk], vaget(lanes), preferred_element_type=F32)
            db = jnp.dot(p_scr[2 * p + 1, :, :tk], vbget(lanes), preferred_element_type=F32)
            acc_s[p] = acc_s[p] + da + db

    def meta_block(mask):
        block(lambda ln: kvm_ref[:, ln], lambda ln: vma_s[:, ln], lambda ln: vmb_s[:, ln],
              trim_ref[...], mask, PAGE)

    col_m = lax.broadcasted_iota(jnp.int32, (tq, PAGE), 1)
    if meta_only:
        row_m = lax.broadcasted_iota(jnp.int32, (tq, PAGE), 0)
        meta_block((col_m < row_m) & (col_m < N_META))
    else:
        def chunk(c, mask):
            rows = pl.ds(pl.multiple_of(c * ATT_BLK, ATT_BLK), ATT_BLK)
            block(lambda ln: k_ref[0, rows, ln], lambda ln: va_s[rows, ln],
                  lambda ln: vb_s[rows, ln], tri_ref[...], mask, ATT_BLK)

        row = lax.broadcasted_iota(jnp.int32, (tq, ATT_BLK), 0)
        col = lax.broadcasted_iota(jnp.int32, (tq, ATT_BLK), 1)
        chunk(j, col < row)

        def body(i, carry):
            chunk(j - 1 - i, None)
            return carry

        lax.fori_loop(0, j, body, 0)
        meta_block(col_m < N_META)

    for p in range(N_HEADS // 2):
        o_ref[0, :, p * LANES:(p + 1) * LANES] = acc_s[p].astype(o_ref.dtype)


def _prompt_attn(kind, q, kv, kv_meta, f_all=None, *, meta_only=False):
    bsz, n, _ = q.shape
    tq = PAGE if meta_only else ATT_BLK
    idx3 = lambda c: (lambda b, j: (b, 0, c))
    in_specs = [pl.BlockSpec((1, tq, ATT_W), lambda b, j: (b, j, 0)),
                pl.BlockSpec((1, n, ATT_W), idx3(0)),
                pl.BlockSpec((1, n, ATT_W), idx3(1)),
                pl.BlockSpec((PAGE, 2 * ATT_W), lambda b, j: (0, 0))]
    vw = ATT_W * (2 if kind == "fox" else 1)
    scratch = [pltpu.VMEM((N_HEADS, tq, LANES), BF16),
               pltpu.VMEM((n, vw), BF16), pltpu.VMEM((n, vw), BF16),
               pltpu.VMEM((PAGE, vw), BF16), pltpu.VMEM((PAGE, vw), BF16)]
    acc = pltpu.VMEM((N_HEADS // 2, tq, LANES), F32)
    if kind == "fox":
        kernel = _fox_kernel
        extra = [f_all]
        in_specs.append(pl.BlockSpec((1, N_HEADS) + f_all.shape[2:], lambda b, j: (b, 0, 0, 0)))
        scratch += [pltpu.VMEM((N_HEADS, tq, LANES), F32), pltpu.VMEM((N_HEADS, tq, LANES), F32), acc,
                    pltpu.VMEM((N_HEADS, tq, LANES), F32),
                    pltpu.VMEM((N_HEADS, tq, ATT_BLK), F32), pltpu.VMEM((N_HEADS, tq, ATT_BLK), BF16)]
    else:
        kernel = _sb_kernel
        extra = [_tri_ones("rev_incl", ATT_BLK)[:, :ATT_BLK], _tri_ones("rev_incl", PAGE)[:, :PAGE]]
        in_specs += [pl.BlockSpec(e.shape, lambda b, j: (0, 0)) for e in extra]
        scratch += [pltpu.VMEM((N_HEADS, tq, LANES), F32), acc,
                    pltpu.VMEM((N_HEADS, tq, ATT_BLK), F32),
                    pltpu.VMEM((N_HEADS, 2 * tq, ATT_BLK), BF16),
                    pltpu.VMEM((N_HEADS, tq, ATT_BLK), BF16)]
    return pl.pallas_call(
        functools.partial(kernel, meta_only=meta_only, tq=tq, n_real=0 if meta_only else n),
        grid=(bsz, n // tq),
        in_specs=in_specs,
        out_specs=pl.BlockSpec((1, tq, ATT_W), lambda b, j: (b, j, 0)),
        out_shape=jax.ShapeDtypeStruct((bsz, n, ATT_W), BF16),
        scratch_shapes=scratch,
        compiler_params=_cparams(2),
        name=kind + ("_meta_attn" if meta_only else "_prompt_attn"),
    )(q, kv, kv, kv_meta, *extra)


def _own_head_mask():
    sub = lax.broadcasted_iota(jnp.int32, (N_HEADS, PAGE_ELEMS), 0)
    lane = lax.broadcasted_iota(jnp.int32, (N_HEADS, PAGE_ELEMS), 1)
    return (lane & (N_HEADS - 1)) == sub


def _page_scores(q, k_pages, s_scr, own):
    for p in range(N_PAGES):
        k2 = k_pages[p][...].reshape(PAGE_ELEMS, HEAD_DIM).astype(BF16)
        st = lax.dot_general(q, k2, _NT, preferred_element_type=F32)
        s_scr[p:p + 1, :] = jnp.sum(jnp.where(own, st, 0.0), axis=0, keepdims=True)


def _page_values(w, v_pages, own):
    acc = jnp.zeros((N_HEADS, HEAD_DIM), F32)
    for p in range(N_PAGES):
        wb = jnp.where(own, jnp.broadcast_to(w[p:p + 1, :], (N_HEADS, PAGE_ELEMS)), 0.0).astype(BF16)
        v2 = v_pages[p][...].reshape(PAGE_ELEMS, HEAD_DIM).astype(BF16)
        acc = acc + jnp.dot(wb, v2, preferred_element_type=F32)
    return acc


def _later_pages(tot, start, c_scr):
    carry = start
    for p in reversed(range(N_PAGES)):
        c_scr[p:p + 1, :] = carry
        carry = carry + tot[p:p + 1, :]
    return c_scr[...]


def _tile_lanes(x, reps=PAGE_ELEMS // LANES):
    return jnp.concatenate([x] * reps, axis=1)


def _fold_lanes(x, op):
    t = x[:, :LANES]
    for k in range(1, x.shape[1] // LANES):
        t = op(t, x[:, k * LANES:(k + 1) * LANES])
    t = (jnp.max if op is jnp.maximum else jnp.sum)(t, axis=0, keepdims=True)
    for shift in (8, 16, 32, 64):
        t = op(t, pltpu.roll(t, shift, 1))
    return t


def _col_to_lanes(col):
    sub = lax.broadcasted_iota(jnp.int32, (N_HEADS, LANES), 0)
    lane = lax.broadcasted_iota(jnp.int32, (N_HEADS, LANES), 1)
    t = jnp.sum(jnp.where(lane == sub, jnp.broadcast_to(col, (N_HEADS, LANES)), 0.0),
                axis=0, keepdims=True)
    for shift in (8, 16, 32, 64):
        t = t + pltpu.roll(t, shift, 1)
    return t


def _lanes_to_col(row):
    sub = lax.broadcasted_iota(jnp.int32, (N_HEADS, LANES), 0)
    lane = lax.broadcasted_iota(jnp.int32, (N_HEADS, LANES), 1)
    return jnp.sum(jnp.where(lane == sub, jnp.broadcast_to(row, (N_HEADS, LANES)), 0.0),
                   axis=1, keepdims=True)


def _sample_fox_kernel(pt_ref, q_ref, kown_ref, vown_ref, lfown_ref, mat_ref, *refs):
    del pt_ref
    k_pages, v_pages, lf_pages = refs[:N_PAGES], refs[N_PAGES:2 * N_PAGES], refs[2 * N_PAGES:3 * N_PAGES]
    o_ref, s_scr, lf_scr, c_scr = refs[3 * N_PAGES:]
    own = _own_head_mask()
    q = q_ref[0]
    _page_scores(q, k_pages, s_scr, own)
    for p in range(N_PAGES):
        lf_scr[p:p + 1, :] = lf_pages[p][...]
    ct = _dot01(lf_scr[...], mat_ref[...])
    later = _later_pages(ct[:, PAGE_ELEMS:], lfown_ref[0], c_scr)
    s = s_scr[...] + (ct[:, :PAGE_ELEMS] + _tile_lanes(later)) * LOG2E
    k_own = kown_ref[0].astype(BF16).astype(F32)
    s_own = _col_to_lanes(jnp.sum(q.astype(F32) * k_own, axis=1, keepdims=True))
    m = jnp.maximum(_fold_lanes(s, jnp.maximum), s_own)
    e = jnp.exp2(s - _tile_lanes(m))
    e_own = jnp.exp2(s_own - m)
    l = _fold_lanes(e, jnp.add) + e_own
    acc = _page_values(e, v_pages, own) + _lanes_to_col(e_own) * vown_ref[0]
    o_ref[0] = (acc / _lanes_to_col(l)).astype(o_ref.dtype)


def _sample_sb_kernel(pt_ref, q_ref, mat_ref, *refs):
    del pt_ref
    k_pages, v_pages = refs[:N_PAGES], refs[N_PAGES:2 * N_PAGES]
    o_ref, z_scr, c_scr = refs[2 * N_PAGES:]
    own = _own_head_mask()
    _page_scores(q_ref[0], k_pages, z_scr, own)
    z = z_scr[...]
    ct = _dot01(_log2_sigmoid_neg(z), mat_ref[...], n_terms=2)
    later = _later_pages(ct[:, PAGE_ELEMS:], jnp.zeros((1, LANES), F32), c_scr)
    w = jnp.exp2(z + ct[:, :PAGE_ELEMS] + _tile_lanes(later))
    o_ref[0] = _page_values(w, v_pages, own).astype(o_ref.dtype)


def _sample_attn(kind, layer, page_table, q, cache_k, cache_v, *, k_own=None, v_own=None, lf_own=None,
                 cache_lf=None):
    db = q.shape[0]
    own3 = pl.BlockSpec((1, N_HEADS, HEAD_DIM), lambda b, pt: (b, 0, 0))
    mat_spec = pl.BlockSpec((PAGE_ELEMS, PAGE_ELEMS + LANES), lambda b, pt: (0, 0))

    def pages(block):
        zeros = (0,) * (len(block) - 2)
        return [pl.BlockSpec(block, lambda b, pt, p=p: (layer, pt[b * N_PAGES + p]) + zeros)
                for p in range(N_PAGES)]

    kv_block = (None, None, PAGE, N_HEADS, HEAD_DIM)
    row_scr = pltpu.VMEM((N_PAGES, PAGE_ELEMS), F32)
    carry_scr = pltpu.VMEM((N_PAGES, LANES), F32)
    if kind == "fox":
        kernel = _sample_fox_kernel
        in_specs = ([own3, own3, own3, pl.BlockSpec((1, 1, LANES), lambda b, pt: (b, 0, 0)), mat_spec]
                    + pages(kv_block) + pages(kv_block) + pages((None, None, 1, PAGE_ELEMS)))
        args = ([q, k_own, v_own, lf_own, _interleaved_cumsum_mat(False)]
                + [cache_k] * N_PAGES + [cache_v] * N_PAGES + [cache_lf] * N_PAGES)
        scratch = [row_scr, row_scr, carry_scr]
    else:
        kernel = _sample_sb_kernel
        in_specs = [own3, mat_spec] + pages(kv_block) + pages(kv_block)
        args = [q, _interleaved_cumsum_mat(True)] + [cache_k] * N_PAGES + [cache_v] * N_PAGES
        scratch = [row_scr, carry_scr]
    return pl.pallas_call(
        kernel,
        grid_spec=pltpu.PrefetchScalarGridSpec(
            num_scalar_prefetch=1, grid=(db,), in_specs=in_specs, out_specs=own3,
            scratch_shapes=scratch),
        out_shape=jax.ShapeDtypeStruct((db, N_HEADS, HEAD_DIM), BF16),
        compiler_params=_cparams(1),
        name="sample_" + kind,
    )(page_table.reshape(-1), *args)


def _block_diag_ssm_weights(bb_re_t, bb_im_t, c_re, c_im):
    eye = jnp.eye(8, dtype=F32)

    def b_blocks(bt):
        b = bt.reshape(SSM_GROUP, 4, 8, SSM_STATE)
        return jnp.einsum("cjgp,gh->jgchp", b, eye).reshape(4, LANES, STATE_W // 4)

    def c_blocks(c):
        c4 = c.reshape(4, 8, SSM_GROUP, SSM_STATE)
        return jnp.einsum("jgcp,gh->jgphc", c4, eye).reshape(4, STATE_W // 4, LANES)

    wb = jnp.concatenate([b_blocks(bb_re_t), b_blocks(bb_im_t)], axis=2).astype(BF16)
    wc = jnp.concatenate([c_blocks(c_re), -c_blocks(c_im)], axis=1).astype(BF16)
    return wb, wc


def kernel(x_prompt, x_sample, cache_fox_k, cache_fox_v, cache_fox_logf, cache_sb_k, cache_sb_v,
           state_ssm_re, state_ssm_im, page_table, meta_tokens, ln_in_g, ln_in_b, w_in, b_forget,
           ssm_a_re, ssm_a_im, ssm_log_dt, ssm_b_re, ssm_b_im, ssm_c_re, ssm_c_im, ssm_d, w_ssm_glu,
           w_fox_o, w_sb_o, w_out, ln1_g, ln1_b, w_ffn_in, w_ffn_out, ln2_g, ln2_b):
    bsz, seq, d = x_prompt.shape
    db = x_sample.shape[0]
    depth = w_in.shape[0]
    d_ff = w_ffn_out.shape[1]
    n_small = db + N_META
    alpha = (2 * depth) ** 0.25
    n_pool = cache_fox_k.shape[1]
    n_chunks = seq // ATT_BLK

    off_fq = SSM_W
    off_fkv = off_fq + ATT_W
    off_f = off_fkv + 2 * ATT_W
    off_sq = off_f + N_HEADS
    off_skv = off_sq + ATT_W
    off_g = off_skv + 2 * ATT_W

    flat = lambda a: a.reshape(depth, STATE_W)
    ldt_rep = jnp.repeat(ssm_log_dt, SSM_STATE, axis=1)
    b_t = lambda a: a.reshape(depth, STATE_W, SSM_GROUP).transpose(0, 2, 1)
    abr, abi, bbr_t, bbi_t = pl.pallas_call(
        _zoh_kernel,
        out_shape=[jax.ShapeDtypeStruct((depth, STATE_W), F32)] * 2
        + [jax.ShapeDtypeStruct((depth, SSM_GROUP, STATE_W), F32)] * 2,
        name="zoh",
    )(flat(ssm_a_re), flat(ssm_a_im), ldt_rep, b_t(ssm_b_re), b_t(ssm_b_im))

    w_in16 = w_in.astype(BF16)
    w_glu16 = w_ssm_glu.astype(BF16)
    w_fox_o16 = w_fox_o.astype(BF16)
    w_sb_o16 = w_sb_o.astype(BF16)
    w_out16 = w_out.astype(BF16)
    w_ffn_in16 = w_ffn_in.astype(BF16)
    w_ffn_out16 = w_ffn_out.astype(BF16)

    clf = cache_fox_logf.astype(F32).reshape(depth, n_pool, 1, PAGE_ELEMS)

    tm_big = 512
    h_big = _ln_rows(x_prompt.reshape(bsz * seq, d), ln_in_g, ln_in_b, tm_big)
    x_small = jnp.concatenate([x_sample.reshape(db, d), meta_tokens.astype(x_prompt.dtype)], axis=0)
    h_small = _ln_rows(x_small, ln_in_g, ln_in_b, n_small)

    outs = {k: [] for k in ("pfk", "pfv", "plf", "psk", "psv", "phr", "phi",
                            "sfk", "sfv", "slf", "ssk", "ssv", "shr", "shi")}

    for l in range(depth):
        wl = w_in16[l]
        w_f = jnp.pad(wl[:, off_f:off_sq], ((0, 0), (0, LANES - N_HEADS)))
        b_f = jnp.pad(b_forget[l].reshape(1, N_HEADS), ((0, 0), (0, LANES - N_HEADS)))
        wb, wc = _block_diag_ssm_weights(bbr_t[l], bbi_t[l], ssm_c_re[l], ssm_c_im[l])
        dvec = ssm_d[l].reshape(1, SSM_W)
        ar_l, ai_l = abr[l:l + 1], abi[l:l + 1]

        def in_proj(h, tm, tb_nb):
            tn = ATT_W
            (u,) = _proj(h, wl[:, :off_fq], [F32], tm=tm, tn=tn, tb_nb=tb_nb, name="in_u")
            (fq,) = _proj(h, wl[:, off_fq:off_fkv], [BF16], tm=tm, tn=tn, scale=Q_SCALE,
                          name="in_fox_q")
            fkv32, fkv16 = _proj(h, wl[:, off_fkv:off_f], [F32, BF16], tm=tm, tn=tn, name="in_fox_kv")
            (lf,) = _proj(h, w_f, [F32], tm=tm, tn=LANES, bias=b_f, act="log_sigmoid", name="in_logf")
            (sq,) = _proj(h, wl[:, off_sq:off_skv], [BF16], tm=tm, tn=tn, scale=Q_SCALE,
                          name="in_sb_q")
            skv32, skv16 = _proj(h, wl[:, off_skv:off_g], [F32, BF16], tm=tm, tn=tn, name="in_sb_kv")
            (gates,) = _proj(h, wl[:, off_g:], [F32], tm=tm, tn=2 * tn, act="sigmoid", name="in_gates")
            return u, fq, fkv32, fkv16, lf[:, :N_HEADS], sq, skv32, skv16, gates

        u_s, fq_s, fkv32_s, fkv16_s, lf_s, sq_s, skv32_s, skv16_s, gates_s = in_proj(h_small, n_small, None)

        y_smp, shr, shi = _ssm(u_s[:db], state_ssm_re[l].reshape(db, STATE_W),
                               state_ssm_im[l].reshape(db, STATE_W), ar_l, ai_l, wb, wc, dvec,
                               nb=db, steps_per_chunk=1)
        zeros8 = jnp.zeros((SUBLANES, STATE_W), F32)
        y_meta, mhr, mhi = _ssm(jnp.repeat(u_s[db:], SUBLANES, axis=0), zeros8, zeros8,
                                ar_l, ai_l, wb, wc, dvec, nb=SUBLANES, steps_per_chunk=N_META)
        y_small = jnp.concatenate([y_smp, y_meta[::SUBLANES]], axis=0)

        u_b, fq_b, fkv32_b, fkv16_b, lf_b, sq_b, skv32_b, skv16_b, gates_b = in_proj(h_big, tm_big, bsz)

        lf_meta_t = jnp.pad(lf_s[db:].T, ((0, 0), (0, ATT_BLK - N_META)))
        lf_real_t = lf_b.reshape(bsz, seq, N_HEADS).transpose(0, 2, 1)
        lf_all = jnp.concatenate(
            [jnp.broadcast_to(lf_meta_t[None], (bsz, N_HEADS, ATT_BLK)), lf_real_t], axis=2)
        f_all = _cumsum_lanes(lf_all.reshape(bsz * N_HEADS, ATT_BLK + seq))
        f_all = f_all.reshape(bsz, N_HEADS, n_chunks + 1, ATT_BLK)

        pad_meta = lambda a: jnp.pad(a[db:], ((0, PAGE - N_META), (0, 0)))
        fkv_meta = pad_meta(fkv16_s)
        skv_meta = pad_meta(skv16_s)

        a_fox_meta = _prompt_attn("fox", pad_meta(fq_s)[None], fkv_meta[None], fkv_meta,
                                  f_all[:1], meta_only=True)[0, :N_META]
        a_sb_meta = _prompt_attn("sb", pad_meta(sq_s)[None], skv_meta[None], skv_meta,
                                 meta_only=True)[0, :N_META]

        heads3 = lambda a: a.reshape(db, N_HEADS, HEAD_DIM)
        a_fox_smp = _sample_attn(
            "fox", l, page_table, heads3(fq_s[:db]), cache_fox_k, cache_fox_v,
            k_own=heads3(fkv32_s[:db, :ATT_W]), v_own=heads3(fkv32_s[:db, ATT_W:]),
            lf_own=jnp.tile(lf_s[:db], (1, LANES // N_HEADS)).reshape(db, 1, LANES), cache_lf=clf)
        a_sb_smp = _sample_attn("sb", l, page_table, heads3(sq_s[:db]), cache_sb_k, cache_sb_v)
        a_fox_small = jnp.concatenate([a_fox_smp.reshape(db, ATT_W), a_fox_meta], axis=0)
        a_sb_small = jnp.concatenate([a_sb_smp.reshape(db, ATT_W), a_sb_meta], axis=0)

        u_b = u_b.reshape(seq * bsz, SSM_W)
        y_big, phr, phi = _ssm(u_b, mhr, mhi, ar_l, ai_l, wb, wc, dvec, nb=bsz, steps_per_chunk=64)
        a_fox_big = _prompt_attn("fox", fq_b.reshape(bsz, seq, ATT_W),
                                 fkv16_b.reshape(bsz, seq, 2 * ATT_W), fkv_meta, f_all)
        a_sb_big = _prompt_attn("sb", sq_b.reshape(bsz, seq, ATT_W),
                                skv16_b.reshape(bsz, seq, 2 * ATT_W), skv_meta)

        def tail(h, y, a_fox, a_sb, gates, nb, tm):
            m = h.shape[0]
            glu = _gated(y, w_glu16[l][:, :d], w_glu16[l][:, d:], F32, tm=tm, tn=d, silu_a=False,
                         name="ssm_glu")
            h1 = _mix_ln(gates, glu.reshape(m // nb, nb * d), a_fox, a_sb, w_fox_o16[l], w_sb_o16[l],
                         w_out16[l], h, ln1_g[l], ln1_b[l], nb=nb, tm=tm, alpha=alpha)
            act = _gated(h1, w_ffn_in16[l][:, :d_ff], w_ffn_in16[l][:, d_ff:], BF16, tm=tm,
                         tn=d_ff // 2, silu_a=True, name="ffn_in")
            return _proj_res_ln(act, w_ffn_out16[l], h1, ln2_g[l], ln2_b[l], tm=tm, alpha=alpha)

        h_small = tail(h_small, y_small, a_fox_small, a_sb_small, gates_s, 1, n_small)
        h_big = tail(h_big, y_big, a_fox_big.reshape(bsz * seq, ATT_W),
                     a_sb_big.reshape(bsz * seq, ATT_W), gates_b, bsz, tm_big)

        def with_meta(real, meta):
            w = real.shape[1]
            return jnp.concatenate(
                [jnp.broadcast_to(meta[None], (bsz, N_META, w)), real.reshape(bsz, seq, w)], axis=1)

        heads = lambda a: a.reshape(a.shape[:-1] + (N_HEADS, HEAD_DIM))
        outs["pfk"].append(heads(with_meta(fkv32_b[:, :ATT_W], fkv32_s[db:, :ATT_W])))
        outs["pfv"].append(heads(with_meta(fkv32_b[:, ATT_W:], fkv32_s[db:, ATT_W:])))
        outs["plf"].append(with_meta(lf_b, lf_s[db:]))
        outs["psk"].append(heads(with_meta(skv32_b[:, :ATT_W], skv32_s[db:, :ATT_W])))
        outs["psv"].append(heads(with_meta(skv32_b[:, ATT_W:], skv32_s[db:, ATT_W:])))
        outs["phr"].append(phr.reshape(bsz, N_GROUPS, SSM_STATE))
        outs["phi"].append(phi.reshape(bsz, N_GROUPS, SSM_STATE))
        outs["sfk"].append(heads(fkv32_s[:db, :ATT_W].reshape(db, 1, ATT_W)))
        outs["sfv"].append(heads(fkv32_s[:db, ATT_W:].reshape(db, 1, ATT_W)))
        outs["slf"].append(lf_s[:db].reshape(db, 1, N_HEADS))
        outs["ssk"].append(heads(skv32_s[:db, :ATT_W].reshape(db, 1, ATT_W)))
        outs["ssv"].append(heads(skv32_s[:db, ATT_W:].reshape(db, 1, ATT_W)))
        outs["shr"].append(shr.reshape(db, N_GROUPS, SSM_STATE))
        outs["shi"].append(shi.reshape(db, N_GROUPS, SSM_STATE))

    st = {k: jnp.stack(v) for k, v in outs.items()}
    return (h_big.reshape(bsz, seq, d), h_small[:db].reshape(db, 1, d),
            st["pfk"], st["pfv"], st["plf"], st["psk"], st["psv"], st["phr"], st["phi"],
            st["sfk"], st["sfv"], st["slf"], st["ssk"], st["ssv"], st["shr"], st["shi"])
```

```python
import functools
import math

import jax
import jax.numpy as jnp
from jax import lax
from jax.experimental import pallas as pl
from jax.experimental.pallas import tpu as pltpu

F32 = jnp.float32
BF16 = jnp.bfloat16

D_MODEL = 1024
N_META = 16
HEAD_DIM = 64
N_HEADS = 8
ATT_W = N_HEADS * HEAD_DIM
SSM_W = 512
SSM_GROUP = 16
N_GROUPS = 32
SSM_STATE = 64
STATE_W = N_GROUPS * SSM_STATE
PAGE = 128
N_PAGES = 16
ATT_BLK = 256
LN_EPS = 1e-5
NEG = -1e30
LOG2E = math.log2(math.e)
Q_SCALE = LOG2E * HEAD_DIM ** -0.5

LANES = 128
SUBLANES = 8
VMEM_LIMIT = 48 * 1024 * 1024

_NT = (((1,), (1,)), ((), ()))


def _cparams(n_axes):
    return pltpu.CompilerParams(
        dimension_semantics=("arbitrary",) * n_axes, vmem_limit_bytes=VMEM_LIMIT)


def _sigmoid(x):
    return 1.0 / (1.0 + jnp.exp(-x))


def _log_sigmoid(x):
    return jnp.minimum(x, 0.0) - jnp.log(1.0 + jnp.exp(-jnp.abs(x)))


def _log2_sigmoid_neg(z2):
    nz = -z2
    return jnp.minimum(nz, 0.0) - jnp.log2(1.0 + jnp.exp2(jnp.minimum(z2, nz)))


def _gelu_tanh(x):
    return 0.5 * x * (1.0 + jnp.tanh(math.sqrt(2.0 / math.pi) * (x + 0.044715 * (x * x * x))))


def _layer_norm(x, g, b):
    mu = jnp.mean(x, axis=-1, keepdims=True)
    xc = x - mu
    var = jnp.mean(xc * xc, axis=-1, keepdims=True)
    return xc * lax.rsqrt(var + LN_EPS) * g + b


def _split_bf16(x, n_terms):
    terms = []
    for _ in range(n_terms - 1):
        t = x.astype(BF16)
        terms.append(t)
        x = x - t.astype(F32)
    terms.append(x.astype(BF16))
    return terms


def _dot01(x, m01, n_terms=3):
    rows = x.shape[0]
    out = jnp.dot(jnp.concatenate(_split_bf16(x, n_terms), axis=0), m01, preferred_element_type=F32)
    acc = out[:rows]
    for t in range(1, n_terms):
        acc = acc + out[t * rows:(t + 1) * rows]
    return acc


def _tri_ones(kind, n=LANES):
    i = jnp.arange(n)[:, None]
    j = jnp.arange(n)[None, :]
    tri = {"fwd_incl": i <= j, "rev_incl": i >= j, "rev_excl": i > j}[kind]
    return jnp.concatenate([tri.astype(BF16), jnp.ones((n, LANES), BF16)], axis=1)


def _ln_kernel(x_ref, g_ref, b_ref, o_ref):
    o_ref[...] = _layer_norm(x_ref[...], g_ref[...], b_ref[...])


def _ln_rows(x, g, b, tm):
    m, d = x.shape
    return pl.pallas_call(
        _ln_kernel,
        grid=(m // tm,),
        in_specs=[pl.BlockSpec((tm, d), lambda i: (i, 0)),
                  pl.BlockSpec((1, d), lambda i: (0, 0)),
                  pl.BlockSpec((1, d), lambda i: (0, 0))],
        out_specs=pl.BlockSpec((tm, d), lambda i: (i, 0)),
        out_shape=jax.ShapeDtypeStruct((m, d), F32),
        compiler_params=_cparams(1),
        name="ln_in",
    )(x, g.reshape(1, d), b.reshape(1, d))


def _proj_kernel(*refs, act, scale, has_bias):
    x_ref, w_ref = refs[0], refs[1]
    pos = 2
    z = jnp.dot(x_ref[...].astype(BF16), w_ref[...], preferred_element_type=F32)
    if has_bias:
        z = z + refs[pos][...]
        pos += 1
    if act == "sigmoid":
        z = _sigmoid(z)
    elif act == "log_sigmoid":
        z = _log_sigmoid(z)
    if scale != 1.0:
        z = z * scale
    for o_ref in refs[pos:]:
        o_ref[...] = z.astype(o_ref.dtype)


def _proj(x, w, out_dtypes, *, tm, tn, act=None, scale=1.0, bias=None, tb_nb=None, name="proj"):
    m, k = x.shape
    n = w.shape[1]
    nj = n // tn
    in_specs = [pl.BlockSpec((tm, k), lambda i, j: (i, 0)),
                pl.BlockSpec((k, tn), lambda i, j: (0, j))]
    args = [x, w]
    if bias is not None:
        in_specs.append(pl.BlockSpec((1, tn), lambda i, j: (0, j)))
        args.append(bias)
    if tb_nb is None:
        out_shape = (m, n)
        out_map = lambda i, j: (i, j)
    else:
        nt = m // tb_nb // tm
        out_shape = (m // tb_nb, tb_nb * n)
        out_map = lambda i, j: (i % nt, (i // nt) * nj + j)
    outs = pl.pallas_call(
        functools.partial(_proj_kernel, act=act, scale=scale, has_bias=bias is not None),
        grid=(m // tm, nj),
        in_specs=in_specs,
        out_specs=[pl.BlockSpec((tm, tn), out_map) for _ in out_dtypes],
        out_shape=[jax.ShapeDtypeStruct(out_shape, dt) for dt in out_dtypes],
        compiler_params=_cparams(2),
        name=name,
    )(*args)
    return outs


def _gated_kernel(x_ref, wa_ref, wb_ref, o_ref, *, silu_a):
    x = x_ref[...].astype(BF16)
    a = jnp.dot(x, wa_ref[...], preferred_element_type=F32)
    b = jnp.dot(x, wb_ref[...], preferred_element_type=F32)
    if silu_a:
        out = a * _sigmoid(a) * b
    else:
        out = a * _sigmoid(b)
    o_ref[...] = out.astype(o_ref.dtype)


def _gated(x, wa, wb, out_dtype, *, tm, tn, silu_a, name):
    m, k = x.shape
    n = wa.shape[1]
    return pl.pallas_call(
        functools.partial(_gated_kernel, silu_a=silu_a),
        grid=(m // tm, n // tn),
        in_specs=[pl.BlockSpec((tm, k), lambda i, j: (i, 0)),
                  pl.BlockSpec((k, tn), lambda i, j: (0, j)),
                  pl.BlockSpec((k, tn), lambda i, j: (0, j))],
        out_specs=pl.BlockSpec((tm, tn), lambda i, j: (i, j)),
        out_shape=jax.ShapeDtypeStruct((m, n), out_dtype),
        compiler_params=_cparams(2),
        name=name,
    )(x, wa, wb)


def _mix_kernel(g_ref, y_ref, af_ref, as_ref, wfo_ref, wso_ref, wout_ref, h_ref, lg_ref, lb_ref,
                o_ref, *, alpha):
    d = D_MODEL
    o_fox = jnp.dot(af_ref[...], wfo_ref[...], preferred_element_type=F32)
    o_sb = jnp.dot(as_ref[...], wso_ref[...], preferred_element_type=F32)
    g = g_ref[...]
    mix = g[:, :d] * y_ref[...] + g[:, d:2 * d] * o_fox + g[:, 2 * d:] * o_sb
    out = jnp.dot(mix.astype(BF16), wout_ref[...], preferred_element_type=F32)
    o_ref[...] = _layer_norm(alpha * h_ref[...] + out, lg_ref[...], lb_ref[...])


def _mix_ln(gates, y_tb, a_fox, a_sb, w_fox_o, w_sb_o, w_out, h, ln_g, ln_b, *, nb, tm, alpha):
    m = h.shape[0]
    n = m // nb
    nt = n // tm
    d = D_MODEL
    bt = lambda w: pl.BlockSpec((tm, w), lambda b, t: (b * nt + t, 0))
    const = lambda r, c: pl.BlockSpec((r, c), lambda b, t: (0, 0))
    return pl.pallas_call(
        functools.partial(_mix_kernel, alpha=alpha),
        grid=(nb, nt),
        in_specs=[bt(3 * d),
                  pl.BlockSpec((tm, d), lambda b, t: (t, b)),
                  bt(ATT_W), bt(ATT_W),
                  const(ATT_W, d), const(ATT_W, d), const(d, d),
                  bt(d), const(1, d), const(1, d)],
        out_specs=bt(d),
        out_shape=jax.ShapeDtypeStruct((m, d), F32),
        compiler_params=_cparams(2),
        name="mix_ln",
    )(gates, y_tb, a_fox, a_sb, w_fox_o, w_sb_o, w_out, h, ln_g.reshape(1, d), ln_b.reshape(1, d))


def _res_ln_kernel(x_ref, w_ref, h_ref, lg_ref, lb_ref, o_ref, *, alpha):
    out = jnp.dot(x_ref[...], w_ref[...], preferred_element_type=F32)
    o_ref[...] = _layer_norm(alpha * h_ref[...] + out, lg_ref[...], lb_ref[...])


def _proj_res_ln(x, w, h, ln_g, ln_b, *, tm, alpha):
    m, k = x.shape
    d = w.shape[1]
    return pl.pallas_call(
        functools.partial(_res_ln_kernel, alpha=alpha),
        grid=(m // tm,),
        in_specs=[pl.BlockSpec((tm, k), lambda i: (i, 0)),
                  pl.BlockSpec((k, d), lambda i: (0, 0)),
                  pl.BlockSpec((tm, d), lambda i: (i, 0)),
                  pl.BlockSpec((1, d), lambda i: (0, 0)),
                  pl.BlockSpec((1, d), lambda i: (0, 0))],
        out_specs=pl.BlockSpec((tm, d), lambda i: (i, 0)),
        out_shape=jax.ShapeDtypeStruct((m, d), F32),
        compiler_params=_cparams(1),
        name="ffn_out_ln",
    )(x, w, h, ln_g.reshape(1, d), ln_b.reshape(1, d))


def _zoh_kernel(ar_ref, ai_ref, ldt_ref, br_ref, bi_ref, abr_ref, abi_ref, bbr_ref, bbi_ref):
    ar, ai = ar_ref[...], ai_ref[...]
    dt = jnp.exp(ldt_ref[...])
    mag = jnp.exp(dt * ar)
    abr = mag * jnp.cos(dt * ai)
    abi = mag * jnp.sin(dt * ai)
    nr, ni = abr - 1.0, abi
    den = ar * ar + ai * ai
    qr = (nr * ar + ni * ai) / den
    qi = (ni * ar - nr * ai) / den
    abr_ref[...] = abr
    abi_ref[...] = abi
    for l in range(ar.shape[0]):
        br, bi = br_ref[l], bi_ref[l]
        bbr_ref[l] = qr[l:l + 1] * br - qi[l:l + 1] * bi
        bbi_ref[l] = qr[l:l + 1] * bi + qi[l:l + 1] * br


def _ssm_kernel(u_ref, h0r_ref, h0i_ref, ar_ref, ai_ref, wb_ref, wc_ref, d_ref,
                y_ref, hr_out_ref, hi_out_ref, bur, bui, sr, si, *, nb, steps):
    @pl.when(pl.program_id(0) == 0)
    def _():
        sr[...] = h0r_ref[...]
        si[...] = h0i_ref[...]

    u = u_ref[...]
    ub = u.astype(BF16)
    n_chunk = SSM_W // LANES
    cw = STATE_W // n_chunk
    for j in range(n_chunk):
        res = jnp.dot(ub[:, j * LANES:(j + 1) * LANES], wb_ref[j], preferred_element_type=F32)
        bur[:, j * cw:(j + 1) * cw] = res[:, :cw]
        bui[:, j * cw:(j + 1) * cw] = res[:, cw:]

    for c in range(n_chunk):
        cs = slice(c * cw, (c + 1) * cw)
        ar = jnp.broadcast_to(ar_ref[:, cs], (nb, cw))
        ai = jnp.broadcast_to(ai_ref[:, cs], (nb, cw))

        def body(t, carry, cs=cs, ar=ar, ai=ai):
            hr, hi = carry
            rows = pl.ds(pl.multiple_of(t * nb, nb), nb)
            nr = ar * hr - ai * hi + bur[rows, cs]
            ni = ar * hi + ai * hr + bui[rows, cs]
            bur[rows, cs] = nr
            bui[rows, cs] = ni
            return nr, ni

        hr, hi = lax.fori_loop(0, steps, body, (sr[:, cs], si[:, cs]))
        sr[:, cs] = hr
        si[:, cs] = hi

    for j in range(n_chunk):
        hrj = bur[:, j * cw:(j + 1) * cw].astype(BF16)
        hij = bui[:, j * cw:(j + 1) * cw].astype(BF16)
        yj = jnp.dot(hrj, wc_ref[j, :cw, :], preferred_element_type=F32)
        yj = yj + jnp.dot(hij, wc_ref[j, cw:, :], preferred_element_type=F32)
        yj = yj + d_ref[:, j * LANES:(j + 1) * LANES] * u[:, j * LANES:(j + 1) * LANES]
        y_ref[:, j * LANES:(j + 1) * LANES] = _gelu_tanh(yj).astype(y_ref.dtype)

    hr_out_ref[...] = sr[...]
    hi_out_ref[...] = si[...]


def _ssm(u_tb, h0r, h0i, abr, abi, wb, wc, dvec, *, nb, steps_per_chunk):
    rows = u_tb.shape[0]
    n_steps = rows // nb
    n_chunks = n_steps // steps_per_chunk
    cr = steps_per_chunk * nb
    const2 = lambda r, c: pl.BlockSpec((r, c), lambda i: (0, 0))
    const3 = lambda a, r, c: pl.BlockSpec((a, r, c), lambda i: (0, 0, 0))
    return pl.pallas_call(
        functools.partial(_ssm_kernel, nb=nb, steps=steps_per_chunk),
        grid=(n_chunks,),
        in_specs=[pl.BlockSpec((cr, SSM_W), lambda i: (i, 0)),
                  const2(nb, STATE_W), const2(nb, STATE_W),
                  const2(1, STATE_W), const2(1, STATE_W),
                  const3(4, LANES, 2 * STATE_W // 4), const3(4, 2 * STATE_W // 4, LANES),
                  const2(1, SSM_W)],
        out_specs=[pl.BlockSpec((cr, SSM_W), lambda i: (i, 0)),
                   const2(nb, STATE_W), const2(nb, STATE_W)],
        out_shape=[jax.ShapeDtypeStruct((rows, SSM_W), BF16),
                   jax.ShapeDtypeStruct((nb, STATE_W), F32),
                   jax.ShapeDtypeStruct((nb, STATE_W), F32)],
        scratch_shapes=[pltpu.VMEM((cr, STATE_W), F32), pltpu.VMEM((cr, STATE_W), F32),
                        pltpu.VMEM((nb, STATE_W), F32), pltpu.VMEM((nb, STATE_W), F32)],
        compiler_params=_cparams(1),
        name="ssm",
    )(u_tb, h0r, h0i, abr, abi, wb, wc, dvec)


def _cumsum_kernel(x_ref, tri_ref, o_ref):
    rows, n = x_ref.shape
    carry = jnp.zeros((rows, LANES), F32)
    for c in range(n // LANES):
        ct = _dot01(x_ref[:, c * LANES:(c + 1) * LANES], tri_ref[...])
        o_ref[:, c * LANES:(c + 1) * LANES] = ct[:, :LANES] + carry
        carry = carry + ct[:, LANES:]


def _cumsum_lanes(x):
    return pl.pallas_call(
        _cumsum_kernel,
        out_shape=jax.ShapeDtypeStruct(x.shape, F32),
        name="cumsum_logf",
    )(x, _tri_ones("fwd_incl"))


def _attn_prologue(q_ref, v_ref, kvm_ref, qm_s, va_s, vb_s, vma_s, vmb_s, *, tq, n_real, ones_cols):
    lane512 = lax.broadcasted_iota(jnp.int32, (1, ATT_W), 1)
    even = ((lane512 >> 6) & 1) == 0

    def fill(dst_a, dst_b, rows, v):
        va = jnp.where(even, v, 0.0).astype(BF16)
        vb = jnp.where(even, 0.0, v).astype(BF16)
        if not ones_cols:
            dst_a[rows, :] = va
            dst_b[rows, :] = vb
            return
        ones = jnp.ones((v.shape[0], LANES), BF16)
        for p in range(N_HEADS // 2):
            src = slice(p * LANES, (p + 1) * LANES)
            for dst, val in ((dst_a, va), (dst_b, vb)):
                dst[rows, 2 * p * LANES:(2 * p + 1) * LANES] = val[:, src]
                dst[rows, (2 * p + 1) * LANES:(2 * p + 2) * LANES] = ones

    @pl.when(pl.program_id(1) == 0)
    def _():
        fill(vma_s, vmb_s, slice(0, PAGE), kvm_ref[:, ATT_W:].astype(F32))
        for r in range(n_real // ATT_BLK):
            rows = slice(r * ATT_BLK, (r + 1) * ATT_BLK)
            fill(va_s, vb_s, rows, v_ref[0, rows, :].astype(F32))

    first_head = lax.broadcasted_iota(jnp.int32, (tq, LANES), 1) < HEAD_DIM
    q = q_ref[0].astype(F32)
    for h in range(N_HEADS):
        qp = q[:, (h // 2) * LANES:(h // 2 + 1) * LANES]
        keep = first_head if h % 2 == 0 else jnp.logical_not(first_head)
        qm_s[h] = jnp.where(keep, qp, 0.0).astype(BF16)
    return first_head


def _fox_kernel(q_ref, k_ref, v_ref, kvm_ref, f_ref, o_ref,
                qm_s, va_s, vb_s, vma_s, vmb_s, m_s, l_s, acc_s, a_s, s_scr, p_scr,
                *, meta_only, tq, n_real):
    j = pl.program_id(1)
    first_head = _attn_prologue(q_ref, v_ref, kvm_ref, qm_s, va_s, vb_s, vma_s, vmb_s,
                                tq=tq, n_real=n_real, ones_cols=True)
    m_s[...] = jnp.full(m_s.shape, NEG, F32)
    l_s[...] = jnp.zeros(l_s.shape, F32)
    acc_s[...] = jnp.zeros(acc_s.shape, F32)
    diag_idx = 0 if meta_only else j + 1
    fref = [f_ref[0, h, pl.ds(diag_idx, 1), :][:, 0:1] for h in range(N_HEADS)]

    def block(kget, vaget, vbget, fidx, mask, tk):
        for p in range(N_HEADS // 2):
            k2 = kget(slice(p * LANES, (p + 1) * LANES))
            s2 = lax.dot_general(qm_s[2 * p:2 * p + 2].reshape(2 * tq, LANES), k2, _NT,
                                 preferred_element_type=F32)
            s_scr[2 * p, :, :tk] = s2[:tq]
            s_scr[2 * p + 1, :, :tk] = s2[tq:]

        def scores(h):
            frow = f_ref[0, h, pl.ds(fidx, 1), :][:, :tk]
            s = s_scr[h, :, :tk] - (frow - fref[h]) * LOG2E
            return s if mask is None else jnp.where(mask, s, NEG)

        for h in range(N_HEADS):
            m_old = m_s[h]
            m_new = jnp.maximum(m_old, jnp.max(scores(h), axis=1, keepdims=True))
            a_s[h] = jnp.exp2(m_old - m_new)
            m_s[h] = m_new
        for h in range(N_HEADS):
            m = m_s[h]
            m = m if tk == LANES else jnp.concatenate([m] * (tk // LANES), axis=1)
            p_scr[h, :, :tk] = jnp.exp2(scores(h) - m).astype(BF16)
        for p in range(N_HEADS // 2):
            aug = slice(2 * p * LANES, (2 * p + 2) * LANES)
            da = jnp.dot(p_scr[2 * p, :, :tk], vaget(aug), preferred_element_type=F32)
            db = jnp.dot(p_scr[2 * p + 1, :, :tk], vbget(aug), preferred_element_type=F32)
            l_s[2 * p] = a_s[2 * p] * l_s[2 * p] + da[:, LANES:]
            l_s[2 * p + 1] = a_s[2 * p + 1] * l_s[2 * p + 1] + db[:, LANES:]
            acc_s[p] = (jnp.where(first_head, a_s[2 * p], a_s[2 * p + 1]) * acc_s[p]
                        + da[:, :LANES] + db[:, :LANES])

    def meta_block(mask):
        block(lambda ln: kvm_ref[:, ln], lambda ln: vma_s[:, ln], lambda ln: vmb_s[:, ln],
              0, mask, PAGE)

    col_m = lax.broadcasted_iota(jnp.int32, (tq, PAGE), 1)
    if meta_only:
        row_m = lax.broadcasted_iota(jnp.int32, (tq, PAGE), 0)
        meta_block((col_m <= row_m) & (col_m < N_META))
    else:
        meta_block(col_m < N_META)

        def chunk(c, mask):
            rows = pl.ds(pl.multiple_of(c * ATT_BLK, ATT_BLK), ATT_BLK)
            block(lambda ln: k_ref[0, rows, ln], lambda ln: va_s[rows, ln],
                  lambda ln: vb_s[rows, ln], c + 1, mask, ATT_BLK)

        def body(c, carry):
            chunk(c, None)
            return carry

        lax.fori_loop(0, j, body, 0)
        row = lax.broadcasted_iota(jnp.int32, (tq, ATT_BLK), 0)
        col = lax.broadcasted_iota(jnp.int32, (tq, ATT_BLK), 1)
        chunk(j, col <= row)

    for p in range(N_HEADS // 2):
        l2 = jnp.where(first_head, l_s[2 * p], l_s[2 * p + 1])
        o_ref[0, :, p * LANES:(p + 1) * LANES] = (acc_s[p] / l2).astype(o_ref.dtype)


def _sb_kernel(q_ref, k_ref, v_ref, kvm_ref, tri_ref, trim_ref, o_ref,
               qm_s, va_s, vb_s, vma_s, vmb_s, r_s, acc_s, z_scr, hl_scr, p_scr,
               *, meta_only, tq, n_real):
    j = pl.program_id(1)
    _attn_prologue(q_ref, v_ref, kvm_ref, qm_s, va_s, vb_s, vma_s, vmb_s, tq=tq, n_real=n_real,
                   ones_cols=False)
    r_s[...] = jnp.zeros(r_s.shape, F32)
    acc_s[...] = jnp.zeros(acc_s.shape, F32)

    def block(kget, vaget, vbget, tri, mask, tk):
        for p in range(N_HEADS // 2):
            k2 = kget(slice(p * LANES, (p + 1) * LANES))
            z2 = lax.dot_general(qm_s[2 * p:2 * p + 2].reshape(2 * tq, LANES), k2, _NT,
                                 preferred_element_type=F32)
            z_scr[2 * p, :, :tk] = z2[:tq]
            z_scr[2 * p + 1, :, :tk] = z2[tq:]
        for h in range(N_HEADS):
            lg = _log2_sigmoid_neg(z_scr[h, :, :tk])
            if mask is not None:
                lg = jnp.where(mask, lg, 0.0)
            hi = lg.astype(BF16)
            hl_scr[h, :tq, :tk] = hi
            hl_scr[h, tq:, :tk] = (lg - hi.astype(F32)).astype(BF16)
        for h in range(N_HEADS):
            out = jnp.dot(hl_scr[h, :, :tk], tri, preferred_element_type=F32)
            ct = out[:tq] + out[tq:]
            r = r_s[h]
            later = r if tk == LANES else jnp.concatenate([r] * (tk // LANES), axis=1)
            w = jnp.exp2(z_scr[h, :, :tk] + ct + later)
            if mask is not None:
                w = jnp.where(mask, w, 0.0)
            p_scr[h, :, :tk] = w.astype(BF16)
            r_s[h] = r + jnp.broadcast_to(ct[:, 0:1], (tq, LANES))
        for p in range(N_HEADS // 2):
            lanes = slice(p * LANES, (p + 1) * LANES)
            da = jnp.dot(p_scr[2 * p, :, :tk], vaget(lanes), preferred_element_type=F32)
            db = jnp.dot(p_scr[2 * p + 1, :, :tk], vbget(lanes), preferred_element_type=F32)
            acc_s[p] = acc_s[p] + da + db

    def meta_block(mask):
        block(lambda ln: kvm_ref[:, ln], lambda ln: vma_s[:, ln], lambda ln: vmb_s[:, ln],
              trim_ref[...], mask, PAGE)

    col_m = lax.broadcasted_iota(jnp.int32, (tq, PAGE), 1)
    if meta_only:
        row_m = lax.broadcasted_iota(jnp.int32, (tq, PAGE), 0)
        meta_block((col_m < row_m) & (col_m < N_META))
    else:
        def chunk(c, mask):
            rows = pl.ds(pl.multiple_of(c * ATT_BLK, ATT_BLK), ATT_BLK)
            block(lambda ln: k_ref[0, rows, ln], lambda ln: va_s[rows, ln],
                  lambda ln: vb_s[rows, ln], tri_ref[...], mask, ATT_BLK)

        row = lax.broadcasted_iota(jnp.int32, (tq, ATT_BLK), 0)
        col = lax.broadcasted_iota(jnp.int32, (tq, ATT_BLK), 1)
        chunk(j, col < row)

        def body(i, carry):
            chunk(j - 1 - i, None)
            return carry

        lax.fori_loop(0, j, body, 0)
        meta_block(col_m < N_META)

    for p in range(N_HEADS // 2):
        o_ref[0, :, p * LANES:(p + 1) * LANES] = acc_s[p].astype(o_ref.dtype)


def _prompt_attn(kind, q, kv, kv_meta, f_all=None, *, meta_only=False):
    bsz, n, _ = q.shape
    tq = PAGE if meta_only else ATT_BLK
    idx3 = lambda c: (lambda b, j: (b, 0, c))
    in_specs = [pl.BlockSpec((1, tq, ATT_W), lambda b, j: (b, j, 0)),
                pl.BlockSpec((1, n, ATT_W), idx3(0)),
                pl.BlockSpec((1, n, ATT_W), idx3(1)),
                pl.BlockSpec((PAGE, 2 * ATT_W), lambda b, j: (0, 0))]
    vw = ATT_W * (2 if kind == "fox" else 1)
    scratch = [pltpu.VMEM((N_HEADS, tq, LANES), BF16),
               pltpu.VMEM((n, vw), BF16), pltpu.VMEM((n, vw), BF16),
               pltpu.VMEM((PAGE, vw), BF16), pltpu.VMEM((PAGE, vw), BF16)]
    acc = pltpu.VMEM((N_HEADS // 2, tq, LANES), F32)
    if kind == "fox":
        kernel = _fox_kernel
        extra = [f_all]
        in_specs.append(pl.BlockSpec((1, N_HEADS) + f_all.shape[2:], lambda b, j: (b, 0, 0, 0)))
        scratch += [pltpu.VMEM((N_HEADS, tq, LANES), F32), pltpu.VMEM((N_HEADS, tq, LANES), F32), acc,
                    pltpu.VMEM((N_HEADS, tq, LANES), F32),
                    pltpu.VMEM((N_HEADS, tq, ATT_BLK), F32), pltpu.VMEM((N_HEADS, tq, ATT_BLK), BF16)]
    else:
        kernel = _sb_kernel
        extra = [_tri_ones("rev_incl", ATT_BLK)[:, :ATT_BLK], _tri_ones("rev_incl", PAGE)[:, :PAGE]]
        in_specs += [pl.BlockSpec(e.shape, lambda b, j: (0, 0)) for e in extra]
        scratch += [pltpu.VMEM((N_HEADS, tq, LANES), F32), acc,
                    pltpu.VMEM((N_HEADS, tq, ATT_BLK), F32),
                    pltpu.VMEM((N_HEADS, 2 * tq, ATT_BLK), BF16),
                    pltpu.VMEM((N_HEADS, tq, ATT_BLK), BF16)]
    return pl.pallas_call(
        functools.partial(kernel, meta_only=meta_only, tq=tq, n_real=0 if meta_only else n),
        grid=(bsz, n // tq),
        in_specs=in_specs,
        out_specs=pl.BlockSpec((1, tq, ATT_W), lambda b, j: (b, j, 0)),
        out_shape=jax.ShapeDtypeStruct((bsz, n, ATT_W), BF16),
        scratch_shapes=scratch,
        compiler_params=_cparams(2),
        name=kind + ("_meta_attn" if meta_only else "_prompt_attn"),
    )(q, kv, kv, kv_meta, *extra)


def _own_head_mask():
    sub = lax.broadcasted_iota(jnp.int32, (N_HEADS, ATT_W), 0)
    lane = lax.broadcasted_iota(jnp.int32, (N_HEADS, ATT_W), 1)
    return lax.shift_right_logical(lane, 6) == sub


def _broadcast_q_columns(q_row, qb_scr):
    r = lax.broadcasted_iota(jnp.int32, (ATT_W, ATT_W), 0)
    c = lax.broadcasted_iota(jnp.int32, (ATT_W, ATT_W), 1)
    qd = jnp.where(r == c, jnp.broadcast_to(q_row, (ATT_W, ATT_W)), 0.0).astype(BF16)
    qb_scr[...] = jnp.dot(qd, jnp.ones((ATT_W, LANES), BF16), preferred_element_type=F32)


def _page_scores(qb_scr, kt_ref):
    sub = lax.broadcasted_iota(jnp.int32, (N_HEADS, PAGE), 0)
    s = jnp.zeros((N_HEADS, PAGE), F32)
    for h in range(N_HEADS):
        t = qb_scr[h * HEAD_DIM:(h + 1) * HEAD_DIM, :] * kt_ref[h]
        s = jnp.where(sub == h, jnp.broadcast_to(jnp.sum(t, axis=0, keepdims=True), (N_HEADS, PAGE)), s)
    return s


def _add_page_values(acc_scr, w, vt_ref):
    for h in range(N_HEADS):
        rows = slice(h * HEAD_DIM, (h + 1) * HEAD_DIM)
        acc_scr[rows, :] = acc_scr[rows, :] + vt_ref[h] * jnp.broadcast_to(w[h:h + 1, :], (HEAD_DIM, PAGE))


def _lane_sums_as_row(acc_scr):
    ones = jnp.ones((SUBLANES, LANES), BF16)
    out = None
    for part in _split_bf16(acc_scr[...], 3):
        d = lax.dot_general(ones, part, _NT, preferred_element_type=F32)
        out = d if out is None else out + d
    return out[0:1, :]


def _sample_fox_kernel(pt_ref, q_ref, kown_ref, vown_ref, lfown_ref, tri_ref, *refs):
    del pt_ref
    kt, vt, lf = refs[:N_PAGES], refs[N_PAGES:2 * N_PAGES], refs[2 * N_PAGES:3 * N_PAGES]
    o_ref, qb_scr, acc_scr = refs[3 * N_PAGES:]
    q_row = q_ref[0].astype(F32)
    _broadcast_q_columns(q_row, qb_scr)
    own_head = _own_head_mask()
    carry = jnp.broadcast_to(lfown_ref[0], (N_HEADS, PAGE))
    scores = [None] * N_PAGES
    for p in reversed(range(N_PAGES)):
        ct = _dot01(lf[p][...], tri_ref[...])
        scores[p] = _page_scores(qb_scr, kt[p]) + (ct[:, :PAGE] + carry) * LOG2E
        carry = carry + ct[:, PAGE:]
    q_bd = jnp.where(own_head, jnp.broadcast_to(q_row, (N_HEADS, ATT_W)), 0.0)
    s_own = jnp.sum(q_bd * kown_ref[0], axis=1, keepdims=True)
    m = s_own
    for p in range(N_PAGES):
        m = jnp.maximum(m, jnp.max(scores[p], axis=1, keepdims=True))
    e_own = jnp.exp2(s_own - m)
    l = e_own
    acc_scr[...] = jnp.zeros(acc_scr.shape, F32)
    for p in range(N_PAGES):
        e = jnp.exp2(scores[p] - m)
        l = l + jnp.sum(e, axis=1, keepdims=True)
        _add_page_values(acc_scr, e, vt[p])
    own_row = jnp.sum(jnp.where(own_head, e_own * vown_ref[0], 0.0), axis=0, keepdims=True)
    l_row = jnp.sum(jnp.where(own_head, jnp.broadcast_to(l, (N_HEADS, ATT_W)), 0.0), axis=0, keepdims=True)
    o_ref[0] = ((_lane_sums_as_row(acc_scr) + own_row) / l_row).astype(o_ref.dtype)


def _sample_sb_kernel(pt_ref, q_ref, tri_ref, *refs):
    del pt_ref
    kt, vt = refs[:N_PAGES], refs[N_PAGES:2 * N_PAGES]
    o_ref, qb_scr, acc_scr = refs[2 * N_PAGES:]
    _broadcast_q_columns(q_ref[0].astype(F32), qb_scr)
    acc_scr[...] = jnp.zeros(acc_scr.shape, F32)
    carry = jnp.zeros((N_HEADS, PAGE), F32)
    for p in reversed(range(N_PAGES)):
        z = _page_scores(qb_scr, kt[p])
        ct = _dot01(_log2_sigmoid_neg(z), tri_ref[...])
        _add_page_values(acc_scr, jnp.exp2(z + ct[:, :PAGE] + carry), vt[p])
        carry = carry + ct[:, PAGE:]
    o_ref[0] = _lane_sums_as_row(acc_scr).astype(o_ref.dtype)


def _sample_attn(kind, layer, page_table, q, cache_kt, cache_vt, *, k_own=None, v_own=None, lf_own=None,
                 cache_lf_t=None):
    db = q.shape[0]
    row3 = pl.BlockSpec((1, 1, ATT_W), lambda b, pt: (b, 0, 0))
    tri_spec = pl.BlockSpec((LANES, 2 * LANES), lambda b, pt: (0, 0))

    def pages(block):
        zeros = (0,) * (len(block) - 2)
        return [pl.BlockSpec(block, lambda b, pt, p=p: (layer, pt[b * N_PAGES + p]) + zeros)
                for p in range(N_PAGES)]

    kv_block = (None, None, N_HEADS, HEAD_DIM, PAGE)
    scratch = [pltpu.VMEM((ATT_W, LANES), F32), pltpu.VMEM((ATT_W, PAGE), F32)]
    if kind == "fox":
        kernel = _sample_fox_kernel
        in_specs = ([row3, row3, row3, pl.BlockSpec((1, N_HEADS, 1), lambda b, pt: (b, 0, 0)), tri_spec]
                    + pages(kv_block) + pages(kv_block) + pages((None, None, N_HEADS, PAGE)))
        args = ([q, k_own, v_own, lf_own, _tri_ones("rev_excl")]
                + [cache_kt] * N_PAGES + [cache_vt] * N_PAGES + [cache_lf_t] * N_PAGES)
    else:
        kernel = _sample_sb_kernel
        in_specs = [row3, tri_spec] + pages(kv_block) + pages(kv_block)
        args = [q, _tri_ones("rev_incl")] + [cache_kt] * N_PAGES + [cache_vt] * N_PAGES
    return pl.pallas_call(
        kernel,
        grid_spec=pltpu.PrefetchScalarGridSpec(
            num_scalar_prefetch=1, grid=(db,), in_specs=in_specs, out_specs=row3,
            scratch_shapes=scratch),
        out_shape=jax.ShapeDtypeStruct((db, 1, ATT_W), BF16),
        compiler_params=_cparams(1),
        name="sample_" + kind,
    )(page_table.reshape(-1), *args)


def _block_diag_ssm_weights(bb_re_t, bb_im_t, c_re, c_im):
    eye = jnp.eye(8, dtype=F32)

    def b_blocks(bt):
        b = bt.reshape(SSM_GROUP, 4, 8, SSM_STATE)
        return jnp.einsum("cjgp,gh->jgchp", b, eye).reshape(4, LANES, STATE_W // 4)

    def c_blocks(c):
        c4 = c.reshape(4, 8, SSM_GROUP, SSM_STATE)
        return jnp.einsum("jgcp,gh->jgphc", c4, eye).reshape(4, STATE_W // 4, LANES)

    wb = jnp.concatenate([b_blocks(bb_re_t), b_blocks(bb_im_t)], axis=2).astype(BF16)
    wc = jnp.concatenate([c_blocks(c_re), -c_blocks(c_im)], axis=1).astype(BF16)
    return wb, wc


def kernel(x_prompt, x_sample, cache_fox_k, cache_fox_v, cache_fox_logf, cache_sb_k, cache_sb_v,
           state_ssm_re, state_ssm_im, page_table, meta_tokens, ln_in_g, ln_in_b, w_in, b_forget,
           ssm_a_re, ssm_a_im, ssm_log_dt, ssm_b_re, ssm_b_im, ssm_c_re, ssm_c_im, ssm_d, w_ssm_glu,
           w_fox_o, w_sb_o, w_out, ln1_g, ln1_b, w_ffn_in, w_ffn_out, ln2_g, ln2_b):
    bsz, seq, d = x_prompt.shape
    db = x_sample.shape[0]
    depth = w_in.shape[0]
    d_ff = w_ffn_out.shape[1]
    n_small = db + N_META
    alpha = (2 * depth) ** 0.25
    n_chunks = seq // ATT_BLK

    off_fq = SSM_W
    off_fkv = off_fq + ATT_W
    off_f = off_fkv + 2 * ATT_W
    off_sq = off_f + N_HEADS
    off_skv = off_sq + ATT_W
    off_g = off_skv + 2 * ATT_W

    flat = lambda a: a.reshape(depth, STATE_W)
    ldt_rep = jnp.repeat(ssm_log_dt, SSM_STATE, axis=1)
    b_t = lambda a: a.reshape(depth, STATE_W, SSM_GROUP).transpose(0, 2, 1)
    abr, abi, bbr_t, bbi_t = pl.pallas_call(
        _zoh_kernel,
        out_shape=[jax.ShapeDtypeStruct((depth, STATE_W), F32)] * 2
        + [jax.ShapeDtypeStruct((depth, SSM_GROUP, STATE_W), F32)] * 2,
        name="zoh",
    )(flat(ssm_a_re), flat(ssm_a_im), ldt_rep, b_t(ssm_b_re), b_t(ssm_b_im))

    w_in16 = w_in.astype(BF16)
    w_glu16 = w_ssm_glu.astype(BF16)
    w_fox_o16 = w_fox_o.astype(BF16)
    w_sb_o16 = w_sb_o.astype(BF16)
    w_out16 = w_out.astype(BF16)
    w_ffn_in16 = w_ffn_in.astype(BF16)
    w_ffn_out16 = w_ffn_out.astype(BF16)

    pos_minor = lambda c: c.transpose(0, 1, 3, 4, 2)
    cache_fox_kt, cache_fox_vt = pos_minor(cache_fox_k), pos_minor(cache_fox_v)
    cache_sb_kt, cache_sb_vt = pos_minor(cache_sb_k), pos_minor(cache_sb_v)
    cache_lf_t = cache_fox_logf.astype(F32).transpose(0, 1, 3, 2)

    tm_big = 512
    h_big = _ln_rows(x_prompt.reshape(bsz * seq, d), ln_in_g, ln_in_b, tm_big)
    x_small = jnp.concatenate([x_sample.reshape(db, d), meta_tokens.astype(x_prompt.dtype)], axis=0)
    h_small = _ln_rows(x_small, ln_in_g, ln_in_b, n_small)

    outs = {k: [] for k in ("pfk", "pfv", "plf", "psk", "psv", "phr", "phi",
                            "sfk", "sfv", "slf", "ssk", "ssv", "shr", "shi")}

    for l in range(depth):
        wl = w_in16[l]
        w_f = jnp.pad(wl[:, off_f:off_sq], ((0, 0), (0, LANES - N_HEADS)))
        b_f = jnp.pad(b_forget[l].reshape(1, N_HEADS), ((0, 0), (0, LANES - N_HEADS)))
        wb, wc = _block_diag_ssm_weights(bbr_t[l], bbi_t[l], ssm_c_re[l], ssm_c_im[l])
        dvec = ssm_d[l].reshape(1, SSM_W)
        ar_l, ai_l = abr[l:l + 1], abi[l:l + 1]

        def in_proj(h, tm, tb_nb):
            tn = ATT_W
            (u,) = _proj(h, wl[:, :off_fq], [F32], tm=tm, tn=tn, tb_nb=tb_nb, name="in_u")
            (fq,) = _proj(h, wl[:, off_fq:off_fkv], [BF16], tm=tm, tn=tn, scale=Q_SCALE,
                          name="in_fox_q")
            fkv32, fkv16 = _proj(h, wl[:, off_fkv:off_f], [F32, BF16], tm=tm, tn=tn, name="in_fox_kv")
            (lf,) = _proj(h, w_f, [F32], tm=tm, tn=LANES, bias=b_f, act="log_sigmoid", name="in_logf")
            (sq,) = _proj(h, wl[:, off_sq:off_skv], [BF16], tm=tm, tn=tn, scale=Q_SCALE,
                          name="in_sb_q")
            skv32, skv16 = _proj(h, wl[:, off_skv:off_g], [F32, BF16], tm=tm, tn=tn, name="in_sb_kv")
            (gates,) = _proj(h, wl[:, off_g:], [F32], tm=tm, tn=2 * tn, act="sigmoid", name="in_gates")
            return u, fq, fkv32, fkv16, lf[:, :N_HEADS], sq, skv32, skv16, gates

        u_s, fq_s, fkv32_s, fkv16_s, lf_s, sq_s, skv32_s, skv16_s, gates_s = in_proj(h_small, n_small, None)

        y_smp, shr, shi = _ssm(u_s[:db], state_ssm_re[l].reshape(db, STATE_W),
                               state_ssm_im[l].reshape(db, STATE_W), ar_l, ai_l, wb, wc, dvec,
                               nb=db, steps_per_chunk=1)
        zeros8 = jnp.zeros((SUBLANES, STATE_W), F32)
        y_meta, mhr, mhi = _ssm(jnp.repeat(u_s[db:], SUBLANES, axis=0), zeros8, zeros8,
                                ar_l, ai_l, wb, wc, dvec, nb=SUBLANES, steps_per_chunk=N_META)
        y_small = jnp.concatenate([y_smp, y_meta[::SUBLANES]], axis=0)

        u_b, fq_b, fkv32_b, fkv16_b, lf_b, sq_b, skv32_b, skv16_b, gates_b = in_proj(h_big, tm_big, bsz)

        lf_meta_t = jnp.pad(lf_s[db:].T, ((0, 0), (0, ATT_BLK - N_META)))
        lf_real_t = lf_b.reshape(bsz, seq, N_HEADS).transpose(0, 2, 1)
        lf_all = jnp.concatenate(
            [jnp.broadcast_to(lf_meta_t[None], (bsz, N_HEADS, ATT_BLK)), lf_real_t], axis=2)
        f_all = _cumsum_lanes(lf_all.reshape(bsz * N_HEADS, ATT_BLK + seq))
        f_all = f_all.reshape(bsz, N_HEADS, n_chunks + 1, ATT_BLK)

        pad_meta = lambda a: jnp.pad(a[db:], ((0, PAGE - N_META), (0, 0)))
        fkv_meta = pad_meta(fkv16_s)
        skv_meta = pad_meta(skv16_s)

        a_fox_meta = _prompt_attn("fox", pad_meta(fq_s)[None], fkv_meta[None], fkv_meta,
                                  f_all[:1], meta_only=True)[0, :N_META]
        a_sb_meta = _prompt_attn("sb", pad_meta(sq_s)[None], skv_meta[None], skv_meta,
                                 meta_only=True)[0, :N_META]

        row3 = lambda a: a.reshape(db, 1, ATT_W)
        a_fox_smp = _sample_attn(
            "fox", l, page_table, row3(fq_s[:db]), cache_fox_kt, cache_fox_vt,
            k_own=row3(fkv32_s[:db, :ATT_W]), v_own=row3(fkv32_s[:db, ATT_W:]),
            lf_own=lf_s[:db].reshape(db, N_HEADS, 1), cache_lf_t=cache_lf_t)
        a_sb_smp = _sample_attn("sb", l, page_table, row3(sq_s[:db]), cache_sb_kt, cache_sb_vt)
        a_fox_small = jnp.concatenate([a_fox_smp.reshape(db, ATT_W), a_fox_meta], axis=0)
        a_sb_small = jnp.concatenate([a_sb_smp.reshape(db, ATT_W), a_sb_meta], axis=0)

        u_b = u_b.reshape(seq * bsz, SSM_W)
        y_big, phr, phi = _ssm(u_b, mhr, mhi, ar_l, ai_l, wb, wc, dvec, nb=bsz, steps_per_chunk=64)
        a_fox_big = _prompt_attn("fox", fq_b.reshape(bsz, seq, ATT_W),
                                 fkv16_b.reshape(bsz, seq, 2 * ATT_W), fkv_meta, f_all)
        a_sb_big = _prompt_attn("sb", sq_b.reshape(bsz, seq, ATT_W),
                                skv16_b.reshape(bsz, seq, 2 * ATT_W), skv_meta)

        def tail(h, y, a_fox, a_sb, gates, nb, tm):
            m = h.shape[0]
            glu = _gated(y, w_glu16[l][:, :d], w_glu16[l][:, d:], F32, tm=tm, tn=d, silu_a=False,
                         name="ssm_glu")
            h1 = _mix_ln(gates, glu.reshape(m // nb, nb * d), a_fox, a_sb, w_fox_o16[l], w_sb_o16[l],
                         w_out16[l], h, ln1_g[l], ln1_b[l], nb=nb, tm=tm, alpha=alpha)
            act = _gated(h1, w_ffn_in16[l][:, :d_ff], w_ffn_in16[l][:, d_ff:], BF16, tm=tm,
                         tn=d_ff // 2, silu_a=True, name="ffn_in")
            return _proj_res_ln(act, w_ffn_out16[l], h1, ln2_g[l], ln2_b[l], tm=tm, alpha=alpha)

        h_small = tail(h_small, y_small, a_fox_small, a_sb_small, gates_s, 1, n_small)
        h_big = tail(h_big, y_big, a_fox_big.reshape(bsz * seq, ATT_W),
                     a_sb_big.reshape(bsz * seq, ATT_W), gates_b, bsz, tm_big)

        def with_meta(real, meta):
            w = real.shape[1]
            return jnp.concatenate(
                [jnp.broadcast_to(meta[None], (bsz, N_META, w)), real.reshape(bsz, seq, w)], axis=1)

        heads = lambda a: a.reshape(a.shape[:-1] + (N_HEADS, HEAD_DIM))
        outs["pfk"].append(heads(with_meta(fkv32_b[:, :ATT_W], fkv32_s[db:, :ATT_W])))
        outs["pfv"].append(heads(with_meta(fkv32_b[:, ATT_W:], fkv32_s[db:, ATT_W:])))
        outs["plf"].append(with_meta(lf_b, lf_s[db:]))
        outs["psk"].append(heads(with_meta(skv32_b[:, :ATT_W], skv32_s[db:, :ATT_W])))
        outs["psv"].append(heads(with_meta(skv32_b[:, ATT_W:], skv32_s[db:, ATT_W:])))
        outs["phr"].append(phr.reshape(bsz, N_GROUPS, SSM_STATE))
        outs["phi"].append(phi.reshape(bsz, N_GROUPS, SSM_STATE))
        outs["sfk"].append(heads(fkv32_s[:db, :ATT_W].reshape(db, 1, ATT_W)))
        outs["sfv"].append(heads(fkv32_s[:db, ATT_W:].reshape(db, 1, ATT_W)))
        outs["slf"].append(lf_s[:db].reshape(db, 1, N_HEADS))
        outs["ssk"].append(heads(skv32_s[:db, :ATT_W].reshape(db, 1, ATT_W)))
        outs["ssv"].append(heads(skv32_s[:db, ATT_W:].reshape(db, 1, ATT_W)))
        outs["shr"].append(shr.reshape(db, N_GROUPS, SSM_STATE))
        outs["shi"].append(shi.reshape(db, N_GROUPS, SSM_STATE))

    st = {k: jnp.stack(v) for k, v in outs.items()}
    return (h_big.reshape(bsz, seq, d), h_small[:db].reshape(db, 1, d),
            st["pfk"], st["pfv"], st["plf"], st["psk"], st["psv"], st["phr"], st["phi"],
            st["sfk"], st["sfv"], st["slf"], st["ssk"], st["ssv"], st["shr"], st["shi"])
```

```python
import functools
import math

import jax
import jax.numpy as jnp
from jax import lax
from jax.experimental import pallas as pl
from jax.experimental.pallas import tpu as pltpu

F32 = jnp.float32
BF16 = jnp.bfloat16

D_MODEL = 1024
N_META = 16
HEAD_DIM = 64
N_HEADS = 8
ATT_W = N_HEADS * HEAD_DIM
SSM_W = 512
SSM_GROUP = 16
N_GROUPS = 32
SSM_STATE = 64
STATE_W = N_GROUPS * SSM_STATE
PAGE = 128
N_PAGES = 16
ATT_BLK = 256
LN_EPS = 1e-5
NEG = -1e30
LOG2E = math.log2(math.e)
Q_SCALE = LOG2E * HEAD_DIM ** -0.5

LANES = 128
SUBLANES = 8
VMEM_LIMIT = 48 * 1024 * 1024

_NT = (((1,), (1,)), ((), ()))


def _cparams(n_axes):
    return pltpu.CompilerParams(
        dimension_semantics=("arbitrary",) * n_axes, vmem_limit_bytes=VMEM_LIMIT)


def _sigmoid(x):
    return 1.0 / (1.0 + jnp.exp(-x))


def _log_sigmoid(x):
    return jnp.minimum(x, 0.0) - jnp.log(1.0 + jnp.exp(-jnp.abs(x)))


def _log2_sigmoid_neg(z2):
    nz = -z2
    return jnp.minimum(nz, 0.0) - jnp.log2(1.0 + jnp.exp2(jnp.minimum(z2, nz)))


def _gelu_tanh(x):
    return 0.5 * x * (1.0 + jnp.tanh(math.sqrt(2.0 / math.pi) * (x + 0.044715 * (x * x * x))))


def _layer_norm(x, g, b):
    mu = jnp.mean(x, axis=-1, keepdims=True)
    xc = x - mu
    var = jnp.mean(xc * xc, axis=-1, keepdims=True)
    return xc * lax.rsqrt(var + LN_EPS) * g + b


def _split_bf16(x, n_terms):
    terms = []
    for _ in range(n_terms - 1):
        t = x.astype(BF16)
        terms.append(t)
        x = x - t.astype(F32)
    terms.append(x.astype(BF16))
    return terms


def _dot01(x, m01, n_terms=3):
    rows = x.shape[0]
    out = jnp.dot(jnp.concatenate(_split_bf16(x, n_terms), axis=0), m01, preferred_element_type=F32)
    acc = out[:rows]
    for t in range(1, n_terms):
        acc = acc + out[t * rows:(t + 1) * rows]
    return acc


def _tri_ones(kind, n=LANES):
    i = jnp.arange(n)[:, None]
    j = jnp.arange(n)[None, :]
    tri = {"fwd_incl": i <= j, "rev_incl": i >= j, "rev_excl": i > j}[kind]
    return jnp.concatenate([tri.astype(BF16), jnp.ones((n, LANES), BF16)], axis=1)


def _ln_kernel(x_ref, g_ref, b_ref, o_ref):
    o_ref[...] = _layer_norm(x_ref[...], g_ref[...], b_ref[...])


def _ln_rows(x, g, b, tm):
    m, d = x.shape
    return pl.pallas_call(
        _ln_kernel,
        grid=(m // tm,),
        in_specs=[pl.BlockSpec((tm, d), lambda i: (i, 0)),
                  pl.BlockSpec((1, d), lambda i: (0, 0)),
                  pl.BlockSpec((1, d), lambda i: (0, 0))],
        out_specs=pl.BlockSpec((tm, d), lambda i: (i, 0)),
        out_shape=jax.ShapeDtypeStruct((m, d), F32),
        compiler_params=_cparams(1),
        name="ln_in",
    )(x, g.reshape(1, d), b.reshape(1, d))


def _proj_kernel(*refs, act, scale, has_bias, t_first):
    x_ref, w_ref = refs[0], refs[1]
    pos = 2
    z = jnp.dot(x_ref[...].astype(BF16), w_ref[...], preferred_element_type=F32)
    if has_bias:
        z = z + refs[pos][...]
        pos += 1
    if act == "sigmoid":
        z = _sigmoid(z)
    elif act == "log_sigmoid":
        z = _log_sigmoid(z)
    if scale != 1.0:
        z = z * scale
    for i, o_ref in enumerate(refs[pos:]):
        if t_first and i == 0:
            o_ref[0] = z.T.astype(o_ref.dtype)
        else:
            o_ref[...] = z.astype(o_ref.dtype)


def _proj(x, w, out_dtypes, *, tm, tn, act=None, scale=1.0, bias=None, tb_nb=None, t_nb=None,
          name="proj"):
    m, k = x.shape
    n = w.shape[1]
    nj = n // tn
    in_specs = [pl.BlockSpec((tm, k), lambda i, j: (i, 0)),
                pl.BlockSpec((k, tn), lambda i, j: (0, j))]
    args = [x, w]
    if bias is not None:
        in_specs.append(pl.BlockSpec((1, tn), lambda i, j: (0, j)))
        args.append(bias)
    if tb_nb is None:
        out_shape = (m, n)
        out_map = lambda i, j: (i, j)
    else:
        nt = m // tb_nb // tm
        out_shape = (m // tb_nb, tb_nb * n)
        out_map = lambda i, j: (i % nt, (i // nt) * nj + j)
    out_specs = [pl.BlockSpec((tm, tn), out_map) for _ in out_dtypes]
    out_shapes = [jax.ShapeDtypeStruct(out_shape, dt) for dt in out_dtypes]
    if t_nb is not None:
        ntt = m // t_nb // tm
        out_specs[0] = pl.BlockSpec((1, tn, tm), lambda i, j: (i // ntt, j, i % ntt))
        out_shapes[0] = jax.ShapeDtypeStruct((t_nb, n, m // t_nb), out_dtypes[0])
    outs = pl.pallas_call(
        functools.partial(_proj_kernel, act=act, scale=scale, has_bias=bias is not None,
                          t_first=t_nb is not None),
        grid=(m // tm, nj),
        in_specs=in_specs,
        out_specs=out_specs,
        out_shape=out_shapes,
        compiler_params=_cparams(2),
        name=name,
    )(*args)
    return outs


def _gated_kernel(x_ref, wa_ref, wb_ref, o_ref, *, silu_a):
    x = x_ref[...].astype(BF16)
    a = jnp.dot(x, wa_ref[...], preferred_element_type=F32)
    b = jnp.dot(x, wb_ref[...], preferred_element_type=F32)
    if silu_a:
        out = a * _sigmoid(a) * b
    else:
        out = a * _sigmoid(b)
    o_ref[...] = out.astype(o_ref.dtype)


def _gated(x, wa, wb, out_dtype, *, tm, tn, silu_a, name):
    m, k = x.shape
    n = wa.shape[1]
    return pl.pallas_call(
        functools.partial(_gated_kernel, silu_a=silu_a),
        grid=(m // tm, n // tn),
        in_specs=[pl.BlockSpec((tm, k), lambda i, j: (i, 0)),
                  pl.BlockSpec((k, tn), lambda i, j: (0, j)),
                  pl.BlockSpec((k, tn), lambda i, j: (0, j))],
        out_specs=pl.BlockSpec((tm, tn), lambda i, j: (i, j)),
        out_shape=jax.ShapeDtypeStruct((m, n), out_dtype),
        compiler_params=_cparams(2),
        name=name,
    )(x, wa, wb)


def _mix_kernel(g_ref, y_ref, af_ref, as_ref, wfo_ref, wso_ref, wout_ref, h_ref, lg_ref, lb_ref,
                o_ref, *, alpha):
    d = D_MODEL
    o_fox = jnp.dot(af_ref[...], wfo_ref[...], preferred_element_type=F32)
    o_sb = jnp.dot(as_ref[...], wso_ref[...], preferred_element_type=F32)
    g = g_ref[...].astype(F32)
    mix = g[:, :d] * y_ref[...].astype(F32) + g[:, d:2 * d] * o_fox + g[:, 2 * d:] * o_sb
    out = jnp.dot(mix.astype(BF16), wout_ref[...], preferred_element_type=F32)
    o_ref[...] = _layer_norm(alpha * h_ref[...] + out, lg_ref[...], lb_ref[...])


def _mix_ln(gates, y_tb, a_fox, a_sb, w_fox_o, w_sb_o, w_out, h, ln_g, ln_b, *, nb, tm, alpha):
    m = h.shape[0]
    n = m // nb
    nt = n // tm
    d = D_MODEL
    bt = lambda w: pl.BlockSpec((tm, w), lambda b, t: (b * nt + t, 0))
    const = lambda r, c: pl.BlockSpec((r, c), lambda b, t: (0, 0))
    return pl.pallas_call(
        functools.partial(_mix_kernel, alpha=alpha),
        grid=(nb, nt),
        in_specs=[bt(3 * d),
                  pl.BlockSpec((tm, d), lambda b, t: (t, b)),
                  bt(ATT_W), bt(ATT_W),
                  const(ATT_W, d), const(ATT_W, d), const(d, d),
                  bt(d), const(1, d), const(1, d)],
        out_specs=bt(d),
        out_shape=jax.ShapeDtypeStruct((m, d), F32),
        compiler_params=_cparams(2),
        name="mix_ln",
    )(gates, y_tb, a_fox, a_sb, w_fox_o, w_sb_o, w_out, h, ln_g.reshape(1, d), ln_b.reshape(1, d))


def _res_ln_kernel(x_ref, w_ref, h_ref, lg_ref, lb_ref, o_ref, *, alpha):
    out = jnp.dot(x_ref[...], w_ref[...], preferred_element_type=F32)
    o_ref[...] = _layer_norm(alpha * h_ref[...] + out, lg_ref[...], lb_ref[...])


def _proj_res_ln(x, w, h, ln_g, ln_b, *, tm, alpha):
    m, k = x.shape
    d = w.shape[1]
    return pl.pallas_call(
        functools.partial(_res_ln_kernel, alpha=alpha),
        grid=(m // tm,),
        in_specs=[pl.BlockSpec((tm, k), lambda i: (i, 0)),
                  pl.BlockSpec((k, d), lambda i: (0, 0)),
                  pl.BlockSpec((tm, d), lambda i: (i, 0)),
                  pl.BlockSpec((1, d), lambda i: (0, 0)),
                  pl.BlockSpec((1, d), lambda i: (0, 0))],
        out_specs=pl.BlockSpec((tm, d), lambda i: (i, 0)),
        out_shape=jax.ShapeDtypeStruct((m, d), F32),
        compiler_params=_cparams(1),
        name="ffn_out_ln",
    )(x, w, h, ln_g.reshape(1, d), ln_b.reshape(1, d))


def _zoh_kernel(ar_ref, ai_ref, ldt_ref, br_ref, bi_ref, abr_ref, abi_ref, bbr_ref, bbi_ref):
    ar, ai = ar_ref[...], ai_ref[...]
    dt = jnp.exp(ldt_ref[...])
    mag = jnp.exp(dt * ar)
    abr = mag * jnp.cos(dt * ai)
    abi = mag * jnp.sin(dt * ai)
    nr, ni = abr - 1.0, abi
    den = ar * ar + ai * ai
    qr = (nr * ar + ni * ai) / den
    qi = (ni * ar - nr * ai) / den
    abr_ref[...] = abr
    abi_ref[...] = abi
    for l in range(ar.shape[0]):
        br, bi = br_ref[l], bi_ref[l]
        bbr_ref[l] = qr[l:l + 1] * br - qi[l:l + 1] * bi
        bbi_ref[l] = qr[l:l + 1] * bi + qi[l:l + 1] * br


def _ssm_kernel(u_ref, h0r_ref, h0i_ref, ar_ref, ai_ref, wb_ref, wc_ref, d_ref,
                y_ref, hr_out_ref, hi_out_ref, bur, bui, sr, si, *, nb, steps):
    @pl.when(pl.program_id(0) == 0)
    def _():
        sr[...] = h0r_ref[...]
        si[...] = h0i_ref[...]

    u = u_ref[...]
    ub = u.astype(BF16)
    n_chunk = SSM_W // LANES
    cw = STATE_W // n_chunk
    for j in range(n_chunk):
        res = jnp.dot(ub[:, j * LANES:(j + 1) * LANES], wb_ref[j], preferred_element_type=F32)
        bur[:, j * cw:(j + 1) * cw] = res[:, :cw]
        bui[:, j * cw:(j + 1) * cw] = res[:, cw:]

    for c in range(n_chunk):
        cs = slice(c * cw, (c + 1) * cw)
        ar = jnp.broadcast_to(ar_ref[:, cs], (nb, cw))
        ai = jnp.broadcast_to(ai_ref[:, cs], (nb, cw))

        def body(t, carry, cs=cs, ar=ar, ai=ai):
            hr, hi = carry
            rows = pl.ds(pl.multiple_of(t * nb, nb), nb)
            nr = ar * hr - ai * hi + bur[rows, cs]
            ni = ar * hi + ai * hr + bui[rows, cs]
            bur[rows, cs] = nr
            bui[rows, cs] = ni
            return nr, ni

        hr, hi = lax.fori_loop(0, steps, body, (sr[:, cs], si[:, cs]))
        sr[:, cs] = hr
        si[:, cs] = hi

    for j in range(n_chunk):
        hrj = bur[:, j * cw:(j + 1) * cw].astype(BF16)
        hij = bui[:, j * cw:(j + 1) * cw].astype(BF16)
        yj = jnp.dot(hrj, wc_ref[j, :cw, :], preferred_element_type=F32)
        yj = yj + jnp.dot(hij, wc_ref[j, cw:, :], preferred_element_type=F32)
        yj = yj + d_ref[:, j * LANES:(j + 1) * LANES] * u[:, j * LANES:(j + 1) * LANES]
        y_ref[:, j * LANES:(j + 1) * LANES] = _gelu_tanh(yj).astype(y_ref.dtype)

    hr_out_ref[...] = sr[...]
    hi_out_ref[...] = si[...]


def _ssm(u_tb, h0r, h0i, abr, abi, wb, wc, dvec, *, nb, steps_per_chunk):
    rows = u_tb.shape[0]
    n_steps = rows // nb
    n_chunks = n_steps // steps_per_chunk
    cr = steps_per_chunk * nb
    const2 = lambda r, c: pl.BlockSpec((r, c), lambda i: (0, 0))
    const3 = lambda a, r, c: pl.BlockSpec((a, r, c), lambda i: (0, 0, 0))
    return pl.pallas_call(
        functools.partial(_ssm_kernel, nb=nb, steps=steps_per_chunk),
        grid=(n_chunks,),
        in_specs=[pl.BlockSpec((cr, SSM_W), lambda i: (i, 0)),
                  const2(nb, STATE_W), const2(nb, STATE_W),
                  const2(1, STATE_W), const2(1, STATE_W),
                  const3(4, LANES, 2 * STATE_W // 4), const3(4, 2 * STATE_W // 4, LANES),
                  const2(1, SSM_W)],
        out_specs=[pl.BlockSpec((cr, SSM_W), lambda i: (i, 0)),
                   const2(nb, STATE_W), const2(nb, STATE_W)],
        out_shape=[jax.ShapeDtypeStruct((rows, SSM_W), BF16),
                   jax.ShapeDtypeStruct((nb, STATE_W), F32),
                   jax.ShapeDtypeStruct((nb, STATE_W), F32)],
        scratch_shapes=[pltpu.VMEM((cr, STATE_W), F32), pltpu.VMEM((cr, STATE_W), F32),
                        pltpu.VMEM((nb, STATE_W), F32), pltpu.VMEM((nb, STATE_W), F32)],
        compiler_params=_cparams(1),
        name="ssm",
    )(u_tb, h0r, h0i, abr, abi, wb, wc, dvec)


def _cumsum_kernel(x_ref, tri_ref, o_ref):
    rows, n = x_ref.shape
    carry = jnp.zeros((rows, LANES), F32)
    for c in range(n // LANES):
        ct = _dot01(x_ref[:, c * LANES:(c + 1) * LANES], tri_ref[...])
        o_ref[:, c * LANES:(c + 1) * LANES] = ct[:, :LANES] + carry
        carry = carry + ct[:, LANES:]


def _cumsum_lanes(x):
    return pl.pallas_call(
        _cumsum_kernel,
        out_shape=jax.ShapeDtypeStruct(x.shape, F32),
        name="cumsum_logf",
    )(x, _tri_ones("fwd_incl"))


def _attn_prologue(q_ref, v_ref, kvm_ref, qm_s, va_s, vb_s, vma_s, vmb_s, *, tq, n_real, ones_cols):
    lane512 = lax.broadcasted_iota(jnp.int32, (1, ATT_W), 1)
    even = ((lane512 >> 6) & 1) == 0

    def fill(dst_a, dst_b, rows, v):
        va = jnp.where(even, v, 0.0).astype(BF16)
        vb = jnp.where(even, 0.0, v).astype(BF16)
        if not ones_cols:
            dst_a[rows, :] = va
            dst_b[rows, :] = vb
            return
        ones = jnp.ones((v.shape[0], LANES), BF16)
        for p in range(N_HEADS // 2):
            src = slice(p * LANES, (p + 1) * LANES)
            for dst, val in ((dst_a, va), (dst_b, vb)):
                dst[rows, 2 * p * LANES:(2 * p + 1) * LANES] = val[:, src]
                dst[rows, (2 * p + 1) * LANES:(2 * p + 2) * LANES] = ones

    @pl.when(pl.program_id(1) == 0)
    def _():
        fill(vma_s, vmb_s, slice(0, PAGE), kvm_ref[:, ATT_W:].astype(F32))
        for r in range(n_real // ATT_BLK):
            rows = slice(r * ATT_BLK, (r + 1) * ATT_BLK)
            fill(va_s, vb_s, rows, v_ref[0, rows, :].astype(F32))

    first_head = lax.broadcasted_iota(jnp.int32, (tq, LANES), 1) < HEAD_DIM
    q = q_ref[0].astype(F32)
    for h in range(N_HEADS):
        qp = q[:, (h // 2) * LANES:(h // 2 + 1) * LANES]
        keep = first_head if h % 2 == 0 else jnp.logical_not(first_head)
        qm_s[h] = jnp.where(keep, qp, 0.0).astype(BF16)
    return first_head


def _fox_kernel(q_ref, k_ref, v_ref, kvm_ref, f_ref, o_ref,
                qm_s, va_s, vb_s, vma_s, vmb_s, m_s, l_s, acc_s, a_s, s_scr, p_scr,
                *, meta_only, tq, n_real):
    j = pl.program_id(1)
    first_head = _attn_prologue(q_ref, v_ref, kvm_ref, qm_s, va_s, vb_s, vma_s, vmb_s,
                                tq=tq, n_real=n_real, ones_cols=True)
    m_s[...] = jnp.full(m_s.shape, NEG, F32)
    l_s[...] = jnp.zeros(l_s.shape, F32)
    acc_s[...] = jnp.zeros(acc_s.shape, F32)
    diag_idx = 0 if meta_only else j + 1
    fref = [f_ref[0, h, pl.ds(diag_idx, 1), :][:, 0:1] for h in range(N_HEADS)]

    def block(kget, vaget, vbget, fidx, mask, tk):
        for p in range(N_HEADS // 2):
            k2 = kget(slice(p * LANES, (p + 1) * LANES))
            s2 = lax.dot_general(qm_s[2 * p:2 * p + 2].reshape(2 * tq, LANES), k2, _NT,
                                 preferred_element_type=F32)
            s_scr[2 * p, :, :tk] = s2[:tq]
            s_scr[2 * p + 1, :, :tk] = s2[tq:]

        def scores(h):
            frow = f_ref[0, h, pl.ds(fidx, 1), :][:, :tk]
            s = s_scr[h, :, :tk] - (frow - fref[h]) * LOG2E
            return s if mask is None else jnp.where(mask, s, NEG)

        for h in range(N_HEADS):
            m_old = m_s[h]
            m_new = jnp.maximum(m_old, jnp.max(scores(h), axis=1, keepdims=True))
            a_s[h] = jnp.exp2(m_old - m_new)
            m_s[h] = m_new
        for h in range(N_HEADS):
            m = m_s[h]
            m = m if tk == LANES else jnp.concatenate([m] * (tk // LANES), axis=1)
            p_scr[h, :, :tk] = jnp.exp2(scores(h) - m).astype(BF16)
        for p in range(N_HEADS // 2):
            aug = slice(2 * p * LANES, (2 * p + 2) * LANES)
            da = jnp.dot(p_scr[2 * p, :, :tk], vaget(aug), preferred_element_type=F32)
            db = jnp.dot(p_scr[2 * p + 1, :, :tk], vbget(aug), preferred_element_type=F32)
            l_s[2 * p] = a_s[2 * p] * l_s[2 * p] + da[:, LANES:]
            l_s[2 * p + 1] = a_s[2 * p + 1] * l_s[2 * p + 1] + db[:, LANES:]
            acc_s[p] = (jnp.where(first_head, a_s[2 * p], a_s[2 * p + 1]) * acc_s[p]
                        + da[:, :LANES] + db[:, :LANES])

    def meta_block(mask):
        block(lambda ln: kvm_ref[:, ln], lambda ln: vma_s[:, ln], lambda ln: vmb_s[:, ln],
              0, mask, PAGE)

    col_m = lax.broadcasted_iota(jnp.int32, (tq, PAGE), 1)
    if meta_only:
        row_m = lax.broadcasted_iota(jnp.int32, (tq, PAGE), 0)
        meta_block((col_m <= row_m) & (col_m < N_META))
    else:
        meta_block(col_m < N_META)

        def chunk(c, mask):
            rows = pl.ds(pl.multiple_of(c * ATT_BLK, ATT_BLK), ATT_BLK)
            block(lambda ln: k_ref[0, rows, ln], lambda ln: va_s[rows, ln],
                  lambda ln: vb_s[rows, ln], c + 1, mask, ATT_BLK)

        def body(c, carry):
            chunk(c, None)
            return carry

        lax.fori_loop(0, j, body, 0)
        row = lax.broadcasted_iota(jnp.int32, (tq, ATT_BLK), 0)
        col = lax.broadcasted_iota(jnp.int32, (tq, ATT_BLK), 1)
        chunk(j, col <= row)

    for p in range(N_HEADS // 2):
        l2 = jnp.where(first_head, l_s[2 * p], l_s[2 * p + 1])
        o_ref[0, :, p * LANES:(p + 1) * LANES] = (acc_s[p] / l2).astype(o_ref.dtype)


def _sb_kernel(q_ref, k_ref, v_ref, kvm_ref, tri_ref, trim_ref, o_ref,
               qm_s, va_s, vb_s, vma_s, vmb_s, r_s, acc_s, z_scr, hl_scr, p_scr,
               *, meta_only, tq, n_real):
    j = pl.program_id(1)
    _attn_prologue(q_ref, v_ref, kvm_ref, qm_s, va_s, vb_s, vma_s, vmb_s, tq=tq, n_real=n_real,
                   ones_cols=False)
    r_s[...] = jnp.zeros(r_s.shape, F32)
    acc_s[...] = jnp.zeros(acc_s.shape, F32)

    def block(kget, vaget, vbget, tri, mask, tk):
        for p in range(N_HEADS // 2):
            k2 = kget(slice(p * LANES, (p + 1) * LANES))
            z2 = lax.dot_general(qm_s[2 * p:2 * p + 2].reshape(2 * tq, LANES), k2, _NT,
                                 preferred_element_type=F32)
            z_scr[2 * p, :, :tk] = z2[:tq]
            z_scr[2 * p + 1, :, :tk] = z2[tq:]
        for h in range(N_HEADS):
            lg = _log2_sigmoid_neg(z_scr[h, :, :tk])
            if mask is not None:
                lg = jnp.where(mask, lg, 0.0)
            hi = lg.astype(BF16)
            hl_scr[h, :tq, :tk] = hi
            hl_scr[h, tq:, :tk] = (lg - hi.astype(F32)).astype(BF16)
        for h in range(N_HEADS):
            ct = (jnp.dot(hl_scr[h, :tq, :tk], tri, preferred_element_type=F32)
                  + jnp.dot(hl_scr[h, tq:, :tk], tri, preferred_element_type=F32))
            r = r_s[h]
            later = r if tk == LANES else jnp.concatenate([r] * (tk // LANES), axis=1)
            w = jnp.exp2(z_scr[h, :, :tk] + ct + later)
            if mask is not None:
                w = jnp.where(mask, w, 0.0)
            p_scr[h, :, :tk] = w.astype(BF16)
            r_s[h] = r + jnp.broadcast_to(ct[:, 0:1], (tq, LANES))
        for p in range(N_HEADS // 2):
            lanes = slice(p * LANES, (p + 1) * LANES)
            da = jnp.dot(p_scr[2 * p, :, :tk], vaget(lanes), preferred_element_type=F32)
            db = jnp.dot(p_scr[2 * p + 1, :, :tk], vbget(lanes), preferred_element_type=F32)
            acc_s[p] = acc_s[p] + da + db

    def meta_block(mask):
        block(lambda ln: kvm_ref[:, ln], lambda ln: vma_s[:, ln], lambda ln: vmb_s[:, ln],
              trim_ref[...], mask, PAGE)

    col_m = lax.broadcasted_iota(jnp.int32, (tq, PAGE), 1)
    if meta_only:
        row_m = lax.broadcasted_iota(jnp.int32, (tq, PAGE), 0)
        meta_block((col_m < row_m) & (col_m < N_META))
    else:
        def chunk(c, mask):
            rows = pl.ds(pl.multiple_of(c * ATT_BLK, ATT_BLK), ATT_BLK)
            block(lambda ln: k_ref[0, rows, ln], lambda ln: va_s[rows, ln],
                  lambda ln: vb_s[rows, ln], tri_ref[...], mask, ATT_BLK)

        row = lax.broadcasted_iota(jnp.int32, (tq, ATT_BLK), 0)
        col = lax.broadcasted_iota(jnp.int32, (tq, ATT_BLK), 1)
        chunk(j, col < row)

        def body(i, carry):
            chunk(j - 1 - i, None)
            return carry

        lax.fori_loop(0, j, body, 0)
        meta_block(col_m < N_META)

    for p in range(N_HEADS // 2):
        o_ref[0, :, p * LANES:(p + 1) * LANES] = acc_s[p].astype(o_ref.dtype)


def _prompt_attn(kind, q, kv, kv_meta, f_all=None, *, meta_only=False):
    bsz, n, _ = q.shape
    tq = PAGE if meta_only else ATT_BLK
    idx3 = lambda c: (lambda b, j: (b, 0, c))
    in_specs = [pl.BlockSpec((1, tq, ATT_W), lambda b, j: (b, j, 0)),
                pl.BlockSpec((1, n, ATT_W), idx3(0)),
                pl.BlockSpec((1, n, ATT_W), idx3(1)),
                pl.BlockSpec((PAGE, 2 * ATT_W), lambda b, j: (0, 0))]
    vw = ATT_W * (2 if kind == "fox" else 1)
    scratch = [pltpu.VMEM((N_HEADS, tq, LANES), BF16),
               pltpu.VMEM((n, vw), BF16), pltpu.VMEM((n, vw), BF16),
               pltpu.VMEM((PAGE, vw), BF16), pltpu.VMEM((PAGE, vw), BF16)]
    acc = pltpu.VMEM((N_HEADS // 2, tq, LANES), F32)
    if kind == "fox":
        kernel = _fox_kernel
        extra = [f_all]
        in_specs.append(pl.BlockSpec((1, N_HEADS) + f_all.shape[2:], lambda b, j: (b, 0, 0, 0)))
        scratch += [pltpu.VMEM((N_HEADS, tq, LANES), F32), pltpu.VMEM((N_HEADS, tq, LANES), F32), acc,
                    pltpu.VMEM((N_HEADS, tq, LANES), F32),
                    pltpu.VMEM((N_HEADS, tq, ATT_BLK), F32), pltpu.VMEM((N_HEADS, tq, ATT_BLK), BF16)]
    else:
        kernel = _sb_kernel
        extra = [_tri_ones("rev_incl", ATT_BLK)[:, :ATT_BLK], _tri_ones("rev_incl", PAGE)[:, :PAGE]]
        in_specs += [pl.BlockSpec(e.shape, lambda b, j: (0, 0)) for e in extra]
        scratch += [pltpu.VMEM((N_HEADS, tq, LANES), F32), acc,
                    pltpu.VMEM((N_HEADS, tq, ATT_BLK), F32),
                    pltpu.VMEM((N_HEADS, 2 * tq, ATT_BLK), BF16),
                    pltpu.VMEM((N_HEADS, tq, ATT_BLK), BF16)]
    return pl.pallas_call(
        functools.partial(kernel, meta_only=meta_only, tq=tq, n_real=0 if meta_only else n),
        grid=(bsz, n // tq),
        in_specs=in_specs,
        out_specs=pl.BlockSpec((1, tq, ATT_W), lambda b, j: (b, j, 0)),
        out_shape=jax.ShapeDtypeStruct((bsz, n, ATT_W), BF16),
        scratch_shapes=scratch,
        compiler_params=_cparams(2),
        name=kind + ("_meta_attn" if meta_only else "_prompt_attn"),
    )(q, kv, kv, kv_meta, *extra)


def _own_head_mask():
    sub = lax.broadcasted_iota(jnp.int32, (N_HEADS, ATT_W), 0)
    lane = lax.broadcasted_iota(jnp.int32, (N_HEADS, ATT_W), 1)
    return lax.shift_right_logical(lane, 6) == sub


def _broadcast_q_columns(q_row, qb_scr):
    r = lax.broadcasted_iota(jnp.int32, (ATT_W, ATT_W), 0)
    c = lax.broadcasted_iota(jnp.int32, (ATT_W, ATT_W), 1)
    qd = jnp.where(r == c, jnp.broadcast_to(q_row, (ATT_W, ATT_W)), 0.0).astype(BF16)
    qb_scr[...] = jnp.dot(qd, jnp.ones((ATT_W, LANES), BF16), preferred_element_type=F32)


def _page_scores(qb_scr, kt_ref):
    sub = lax.broadcasted_iota(jnp.int32, (N_HEADS, PAGE), 0)
    s = jnp.zeros((N_HEADS, PAGE), F32)
    for h in range(N_HEADS):
        t = qb_scr[h * HEAD_DIM:(h + 1) * HEAD_DIM, :] * kt_ref[h]
        s = jnp.where(sub == h, jnp.broadcast_to(jnp.sum(t, axis=0, keepdims=True), (N_HEADS, PAGE)), s)
    return s


def _add_page_values(acc_scr, w, vt_ref):
    for h in range(N_HEADS):
        rows = slice(h * HEAD_DIM, (h + 1) * HEAD_DIM)
        acc_scr[rows, :] = acc_scr[rows, :] + vt_ref[h] * jnp.broadcast_to(w[h:h + 1, :], (HEAD_DIM, PAGE))


def _lane_sums_as_row(acc_scr):
    ones = jnp.ones((SUBLANES, LANES), BF16)
    out = None
    for part in _split_bf16(acc_scr[...], 3):
        d = lax.dot_general(ones, part, _NT, preferred_element_type=F32)
        out = d if out is None else out + d
    return out[0:1, :]


def _sample_fox_kernel(pt_ref, q_ref, kown_ref, vown_ref, lfown_ref, tri_ref, *refs):
    del pt_ref
    kt, vt, lf = refs[:N_PAGES], refs[N_PAGES:2 * N_PAGES], refs[2 * N_PAGES:3 * N_PAGES]
    o_ref, qb_scr, acc_scr = refs[3 * N_PAGES:]
    q_row = q_ref[0].astype(F32)
    _broadcast_q_columns(q_row, qb_scr)
    own_head = _own_head_mask()
    carry = jnp.broadcast_to(lfown_ref[0], (N_HEADS, PAGE))
    scores = [None] * N_PAGES
    for p in reversed(range(N_PAGES)):
        ct = _dot01(lf[p][...], tri_ref[...])
        scores[p] = _page_scores(qb_scr, kt[p]) + (ct[:, :PAGE] + carry) * LOG2E
        carry = carry + ct[:, PAGE:]
    q_bd = jnp.where(own_head, jnp.broadcast_to(q_row, (N_HEADS, ATT_W)), 0.0)
    s_own = jnp.sum(q_bd * kown_ref[0], axis=1, keepdims=True)
    m = s_own
    for p in range(N_PAGES):
        m = jnp.maximum(m, jnp.max(scores[p], axis=1, keepdims=True))
    e_own = jnp.exp2(s_own - m)
    l = e_own
    acc_scr[...] = jnp.zeros(acc_scr.shape, F32)
    for p in range(N_PAGES):
        e = jnp.exp2(scores[p] - m)
        l = l + jnp.sum(e, axis=1, keepdims=True)
        _add_page_values(acc_scr, e, vt[p])
    own_row = jnp.sum(jnp.where(own_head, e_own * vown_ref[0], 0.0), axis=0, keepdims=True)
    l_row = jnp.sum(jnp.where(own_head, jnp.broadcast_to(l, (N_HEADS, ATT_W)), 0.0), axis=0, keepdims=True)
    o_ref[0] = ((_lane_sums_as_row(acc_scr) + own_row) / l_row).astype(o_ref.dtype)


def _sample_sb_kernel(pt_ref, q_ref, tri_ref, *refs):
    del pt_ref
    kt, vt = refs[:N_PAGES], refs[N_PAGES:2 * N_PAGES]
    o_ref, qb_scr, acc_scr = refs[2 * N_PAGES:]
    _broadcast_q_columns(q_ref[0].astype(F32), qb_scr)
    acc_scr[...] = jnp.zeros(acc_scr.shape, F32)
    carry = jnp.zeros((N_HEADS, PAGE), F32)
    for p in reversed(range(N_PAGES)):
        z = _page_scores(qb_scr, kt[p])
        ct = _dot01(_log2_sigmoid_neg(z), tri_ref[...])
        _add_page_values(acc_scr, jnp.exp2(z + ct[:, :PAGE] + carry), vt[p])
        carry = carry + ct[:, PAGE:]
    o_ref[0] = _lane_sums_as_row(acc_scr).astype(o_ref.dtype)


def _sample_attn(kind, layer, page_table, q, cache_kt, cache_vt, *, k_own=None, v_own=None, lf_own=None,
                 cache_lf_t=None):
    db = q.shape[0]
    row3 = pl.BlockSpec((1, 1, ATT_W), lambda b, pt: (b, 0, 0))
    tri_spec = pl.BlockSpec((LANES, 2 * LANES), lambda b, pt: (0, 0))

    def pages(block):
        zeros = (0,) * (len(block) - 2)
        return [pl.BlockSpec(block, lambda b, pt, p=p: (layer, pt[b * N_PAGES + p]) + zeros)
                for p in range(N_PAGES)]

    kv_block = (None, None, N_HEADS, HEAD_DIM, PAGE)
    scratch = [pltpu.VMEM((ATT_W, LANES), F32), pltpu.VMEM((ATT_W, PAGE), F32)]
    if kind == "fox":
        kernel = _sample_fox_kernel
        in_specs = ([row3, row3, row3, pl.BlockSpec((1, N_HEADS, 1), lambda b, pt: (b, 0, 0)), tri_spec]
                    + pages(kv_block) + pages(kv_block) + pages((None, None, N_HEADS, PAGE)))
        args = ([q, k_own, v_own, lf_own, _tri_ones("rev_excl")]
                + [cache_kt] * N_PAGES + [cache_vt] * N_PAGES + [cache_lf_t] * N_PAGES)
    else:
        kernel = _sample_sb_kernel
        in_specs = [row3, tri_spec] + pages(kv_block) + pages(kv_block)
        args = [q, _tri_ones("rev_incl")] + [cache_kt] * N_PAGES + [cache_vt] * N_PAGES
    return pl.pallas_call(
        kernel,
        grid_spec=pltpu.PrefetchScalarGridSpec(
            num_scalar_prefetch=1, grid=(db,), in_specs=in_specs, out_specs=row3,
            scratch_shapes=scratch),
        out_shape=jax.ShapeDtypeStruct((db, 1, ATT_W), BF16),
        compiler_params=_cparams(1),
        name="sample_" + kind,
    )(page_table.reshape(-1), *args)


def _block_diag_ssm_weights(bb_re_t, bb_im_t, c_re, c_im):
    eye = jnp.eye(8, dtype=F32)

    def b_blocks(bt):
        b = bt.reshape(SSM_GROUP, 4, 8, SSM_STATE)
        return jnp.einsum("cjgp,gh->jgchp", b, eye).reshape(4, LANES, STATE_W // 4)

    def c_blocks(c):
        c4 = c.reshape(4, 8, SSM_GROUP, SSM_STATE)
        return jnp.einsum("jgcp,gh->jgphc", c4, eye).reshape(4, STATE_W // 4, LANES)

    wb = jnp.concatenate([b_blocks(bb_re_t), b_blocks(bb_im_t)], axis=2).astype(BF16)
    wc = jnp.concatenate([c_blocks(c_re), -c_blocks(c_im)], axis=1).astype(BF16)
    return wb, wc


def kernel(x_prompt, x_sample, cache_fox_k, cache_fox_v, cache_fox_logf, cache_sb_k, cache_sb_v,
           state_ssm_re, state_ssm_im, page_table, meta_tokens, ln_in_g, ln_in_b, w_in, b_forget,
           ssm_a_re, ssm_a_im, ssm_log_dt, ssm_b_re, ssm_b_im, ssm_c_re, ssm_c_im, ssm_d, w_ssm_glu,
           w_fox_o, w_sb_o, w_out, ln1_g, ln1_b, w_ffn_in, w_ffn_out, ln2_g, ln2_b):
    bsz, seq, d = x_prompt.shape
    db = x_sample.shape[0]
    depth = w_in.shape[0]
    d_ff = w_ffn_out.shape[1]
    n_small = db + N_META
    alpha = (2 * depth) ** 0.25
    n_chunks = seq // ATT_BLK

    off_fq = SSM_W
    off_fkv = off_fq + ATT_W
    off_f = off_fkv + 2 * ATT_W
    off_sq = off_f + N_HEADS
    off_skv = off_sq + ATT_W
    off_g = off_skv + 2 * ATT_W

    flat = lambda a: a.reshape(depth, STATE_W)
    ldt_rep = jnp.repeat(ssm_log_dt, SSM_STATE, axis=1)
    b_t = lambda a: a.reshape(depth, STATE_W, SSM_GROUP).transpose(0, 2, 1)
    abr, abi, bbr_t, bbi_t = pl.pallas_call(
        _zoh_kernel,
        out_shape=[jax.ShapeDtypeStruct((depth, STATE_W), F32)] * 2
        + [jax.ShapeDtypeStruct((depth, SSM_GROUP, STATE_W), F32)] * 2,
        name="zoh",
    )(flat(ssm_a_re), flat(ssm_a_im), ldt_rep, b_t(ssm_b_re), b_t(ssm_b_im))

    w_in16 = w_in.astype(BF16)
    w_glu16 = w_ssm_glu.astype(BF16)
    w_fox_o16 = w_fox_o.astype(BF16)
    w_sb_o16 = w_sb_o.astype(BF16)
    w_out16 = w_out.astype(BF16)
    w_ffn_in16 = w_ffn_in.astype(BF16)
    w_ffn_out16 = w_ffn_out.astype(BF16)

    pos_minor = lambda c: c.transpose(0, 1, 3, 4, 2)
    cache_fox_kt, cache_fox_vt = pos_minor(cache_fox_k), pos_minor(cache_fox_v)
    cache_sb_kt, cache_sb_vt = pos_minor(cache_sb_k), pos_minor(cache_sb_v)
    cache_lf_t = cache_fox_logf.astype(F32).transpose(0, 1, 3, 2)

    tm_big = 512
    h_big = _ln_rows(x_prompt.reshape(bsz * seq, d), ln_in_g, ln_in_b, tm_big)
    x_small = jnp.concatenate([x_sample.reshape(db, d), meta_tokens.astype(x_prompt.dtype)], axis=0)
    h_small = _ln_rows(x_small, ln_in_g, ln_in_b, n_small)

    outs = {k: [] for k in ("pfk", "pfv", "plf", "psk", "psv", "phr", "phi",
                            "sfk", "sfv", "slf", "ssk", "ssv", "shr", "shi")}

    for l in range(depth):
        wl = w_in16[l]
        w_f = jnp.pad(wl[:, off_f:off_sq], ((0, 0), (0, LANES - N_HEADS)))
        b_f = jnp.pad(b_forget[l].reshape(1, N_HEADS), ((0, 0), (0, LANES - N_HEADS)))
        wb, wc = _block_diag_ssm_weights(bbr_t[l], bbi_t[l], ssm_c_re[l], ssm_c_im[l])
        dvec = ssm_d[l].reshape(1, SSM_W)
        ar_l, ai_l = abr[l:l + 1], abi[l:l + 1]

        def in_proj(h, tm, nb):
            tn = ATT_W
            (u,) = _proj(h, wl[:, :off_fq], [F32], tm=tm, tn=tn, tb_nb=nb, name="in_u")
            (fq,) = _proj(h, wl[:, off_fq:off_fkv], [BF16], tm=tm, tn=tn, scale=Q_SCALE,
                          name="in_fox_q")
            fkv32, fkv16 = _proj(h, wl[:, off_fkv:off_f], [F32, BF16], tm=tm, tn=tn, t_nb=nb,
                                 name="in_fox_kv")
            (lf,) = _proj(h, w_f, [F32], tm=tm, tn=LANES, bias=b_f, act="log_sigmoid", name="in_logf")
            (sq,) = _proj(h, wl[:, off_sq:off_skv], [BF16], tm=tm, tn=tn, scale=Q_SCALE,
                          name="in_sb_q")
            skv32, skv16 = _proj(h, wl[:, off_skv:off_g], [F32, BF16], tm=tm, tn=tn, t_nb=nb,
                                 name="in_sb_kv")
            (gates,) = _proj(h, wl[:, off_g:], [BF16], tm=tm, tn=2 * tn, act="sigmoid", name="in_gates")
            return u, fq, fkv32, fkv16, lf[:, :N_HEADS], sq, skv32, skv16, gates

        u_s, fq_s, fkv32_s, fkv16_s, lf_s, sq_s, skv32_s, skv16_s, gates_s = in_proj(h_small, n_small, None)

        y_smp, shr, shi = _ssm(u_s[:db], state_ssm_re[l].reshape(db, STATE_W),
                               state_ssm_im[l].reshape(db, STATE_W), ar_l, ai_l, wb, wc, dvec,
                               nb=db, steps_per_chunk=1)
        zeros8 = jnp.zeros((SUBLANES, STATE_W), F32)
        y_meta, mhr, mhi = _ssm(jnp.repeat(u_s[db:], SUBLANES, axis=0), zeros8, zeros8,
                                ar_l, ai_l, wb, wc, dvec, nb=SUBLANES, steps_per_chunk=N_META)
        y_small = jnp.concatenate([y_smp, y_meta[::SUBLANES]], axis=0)

        u_b, fq_b, fkv32_b, fkv16_b, lf_b, sq_b, skv32_b, skv16_b, gates_b = in_proj(h_big, tm_big, bsz)

        lf_meta_t = jnp.pad(lf_s[db:].T, ((0, 0), (0, ATT_BLK - N_META)))
        lf_real_t = lf_b.reshape(bsz, seq, N_HEADS).transpose(0, 2, 1)
        lf_all = jnp.concatenate(
            [jnp.broadcast_to(lf_meta_t[None], (bsz, N_HEADS, ATT_BLK)), lf_real_t], axis=2)
        f_all = _cumsum_lanes(lf_all.reshape(bsz * N_HEADS, ATT_BLK + seq))
        f_all = f_all.reshape(bsz, N_HEADS, n_chunks + 1, ATT_BLK)

        pad_meta = lambda a: jnp.pad(a[db:], ((0, PAGE - N_META), (0, 0)))
        fkv_meta = pad_meta(fkv16_s)
        skv_meta = pad_meta(skv16_s)

        a_fox_meta = _prompt_attn("fox", pad_meta(fq_s)[None], fkv_meta[None], fkv_meta,
                                  f_all[:1], meta_only=True)[0, :N_META]
        a_sb_meta = _prompt_attn("sb", pad_meta(sq_s)[None], skv_meta[None], skv_meta,
                                 meta_only=True)[0, :N_META]

        row3 = lambda a: a.reshape(db, 1, ATT_W)
        a_fox_smp = _sample_attn(
            "fox", l, page_table, row3(fq_s[:db]), cache_fox_kt, cache_fox_vt,
            k_own=row3(fkv32_s[:db, :ATT_W]), v_own=row3(fkv32_s[:db, ATT_W:]),
            lf_own=lf_s[:db].reshape(db, N_HEADS, 1), cache_lf_t=cache_lf_t)
        a_sb_smp = _sample_attn("sb", l, page_table, row3(sq_s[:db]), cache_sb_kt, cache_sb_vt)
        a_fox_small = jnp.concatenate([a_fox_smp.reshape(db, ATT_W), a_fox_meta], axis=0)
        a_sb_small = jnp.concatenate([a_sb_smp.reshape(db, ATT_W), a_sb_meta], axis=0)

        u_b = u_b.reshape(seq * bsz, SSM_W)
        y_big, phr, phi = _ssm(u_b, mhr, mhi, ar_l, ai_l, wb, wc, dvec, nb=bsz, steps_per_chunk=64)
        a_fox_big = _prompt_attn("fox", fq_b.reshape(bsz, seq, ATT_W),
                                 fkv16_b.reshape(bsz, seq, 2 * ATT_W), fkv_meta, f_all)
        a_sb_big = _prompt_attn("sb", sq_b.reshape(bsz, seq, ATT_W),
                                skv16_b.reshape(bsz, seq, 2 * ATT_W), skv_meta)

        def tail(h, y, a_fox, a_sb, gates, nb, tm):
            m = h.shape[0]
            glu = _gated(y, w_glu16[l][:, :d], w_glu16[l][:, d:], BF16, tm=tm, tn=d, silu_a=False,
                         name="ssm_glu")
            h1 = _mix_ln(gates, glu.reshape(m // nb, nb * d), a_fox, a_sb, w_fox_o16[l], w_sb_o16[l],
                         w_out16[l], h, ln1_g[l], ln1_b[l], nb=nb, tm=tm, alpha=alpha)
            act = _gated(h1, w_ffn_in16[l][:, :d_ff], w_ffn_in16[l][:, d_ff:], BF16, tm=tm,
                         tn=d_ff // 2, silu_a=True, name="ffn_in")
            return _proj_res_ln(act, w_ffn_out16[l], h1, ln2_g[l], ln2_b[l], tm=tm, alpha=alpha)

        h_small = tail(h_small, y_small, a_fox_small, a_sb_small, gates_s, 1, n_small)
        h_big = tail(h_big, y_big, a_fox_big.reshape(bsz * seq, ATT_W),
                     a_sb_big.reshape(bsz * seq, ATT_W), gates_b, bsz, tm_big)

        def with_meta(real, meta):
            w = real.shape[1]
            return jnp.concatenate(
                [jnp.broadcast_to(meta[None], (bsz, N_META, w)), real.reshape(bsz, seq, w)], axis=1)

        def with_meta_t(real_t, meta):
            full = jnp.concatenate(
                [jnp.broadcast_to(meta.T[None], (bsz, ATT_W, N_META)), real_t], axis=2)
            return full.reshape(bsz, N_HEADS, HEAD_DIM, N_META + seq).transpose(0, 3, 1, 2)

        heads = lambda a: a.reshape(a.shape[:-1] + (N_HEADS, HEAD_DIM))
        outs["pfk"].append(with_meta_t(fkv32_b[:, :ATT_W], fkv32_s[db:, :ATT_W]))
        outs["pfv"].append(with_meta_t(fkv32_b[:, ATT_W:], fkv32_s[db:, ATT_W:]))
        outs["plf"].append(with_meta(lf_b, lf_s[db:]))
        outs["psk"].append(with_meta_t(skv32_b[:, :ATT_W], skv32_s[db:, :ATT_W]))
        outs["psv"].append(with_meta_t(skv32_b[:, ATT_W:], skv32_s[db:, ATT_W:]))
        outs["phr"].append(phr.reshape(bsz, N_GROUPS, SSM_STATE))
        outs["phi"].append(phi.reshape(bsz, N_GROUPS, SSM_STATE))
        outs["sfk"].append(heads(fkv32_s[:db, :ATT_W].reshape(db, 1, ATT_W)))
        outs["sfv"].append(heads(fkv32_s[:db, ATT_W:].reshape(db, 1, ATT_W)))
        outs["slf"].append(lf_s[:db].reshape(db, 1, N_HEADS))
        outs["ssk"].append(heads(skv32_s[:db, :ATT_W].reshape(db, 1, ATT_W)))
        outs["ssv"].append(heads(skv32_s[:db, ATT_W:].reshape(db, 1, ATT_W)))
        outs["shr"].append(shr.reshape(db, N_GROUPS, SSM_STATE))
        outs["shi"].append(shi.reshape(db, N_GROUPS, SSM_STATE))

    st = {k: jnp.stack(v) for k, v in outs.items()}
    return (h_big.reshape(bsz, seq, d), h_small[:db].reshape(db, 1, d),
            st["pfk"], st["pfv"], st["plf"], st["psk"], st["psv"], st["phr"], st["phi"],
            st["sfk"], st["sfv"], st["slf"], st["ssk"], st["ssv"], st["shr"], st["shi"])
```

```python
import functools
import math

import jax
import jax.numpy as jnp
from jax import lax
from jax.experimental import pallas as pl
from jax.experimental.pallas import tpu as pltpu

F32 = jnp.float32
BF16 = jnp.bfloat16

D_MODEL = 1024
N_META = 16
HEAD_DIM = 64
N_HEADS = 8
ATT_W = N_HEADS * HEAD_DIM
SSM_W = 512
SSM_GROUP = 16
N_GROUPS = 32
SSM_STATE = 64
STATE_W = N_GROUPS * SSM_STATE
PAGE = 128
N_PAGES = 16
ATT_BLK = 256
LN_EPS = 1e-5
NEG = -1e30
LOG2E = math.log2(math.e)
Q_SCALE = LOG2E * HEAD_DIM ** -0.5

LANES = 128
SUBLANES = 8
VMEM_LIMIT = 48 * 1024 * 1024

_NT = (((1,), (1,)), ((), ()))


def _cparams(n_axes):
    return pltpu.CompilerParams(
        dimension_semantics=("arbitrary",) * n_axes, vmem_limit_bytes=VMEM_LIMIT)


def _sigmoid(x):
    return 1.0 / (1.0 + jnp.exp(-x))


def _log_sigmoid(x):
    return jnp.minimum(x, 0.0) - jnp.log(1.0 + jnp.exp(-jnp.abs(x)))


def _log2_sigmoid_neg(z2):
    nz = -z2
    return jnp.minimum(nz, 0.0) - jnp.log2(1.0 + jnp.exp2(jnp.minimum(z2, nz)))


def _gelu_tanh(x):
    return 0.5 * x * (1.0 + jnp.tanh(math.sqrt(2.0 / math.pi) * (x + 0.044715 * (x * x * x))))


def _layer_norm(x, g, b):
    mu = jnp.mean(x, axis=-1, keepdims=True)
    xc = x - mu
    var = jnp.mean(xc * xc, axis=-1, keepdims=True)
    return xc * lax.rsqrt(var + LN_EPS) * g + b


def _split_bf16(x, n_terms):
    terms = []
    for _ in range(n_terms - 1):
        t = x.astype(BF16)
        terms.append(t)
        x = x - t.astype(F32)
    terms.append(x.astype(BF16))
    return terms


def _dot01(x, m01, n_terms=3):
    rows = x.shape[0]
    out = jnp.dot(jnp.concatenate(_split_bf16(x, n_terms), axis=0), m01, preferred_element_type=F32)
    acc = out[:rows]
    for t in range(1, n_terms):
        acc = acc + out[t * rows:(t + 1) * rows]
    return acc


def _tri_ones(kind, n=LANES):
    i = jnp.arange(n)[:, None]
    j = jnp.arange(n)[None, :]
    tri = {"fwd_incl": i <= j, "rev_incl": i >= j, "rev_excl": i > j}[kind]
    return jnp.concatenate([tri.astype(BF16), jnp.ones((n, LANES), BF16)], axis=1)


def _ln_kernel(x_ref, g_ref, b_ref, o_ref):
    o_ref[...] = _layer_norm(x_ref[...], g_ref[...], b_ref[...])


def _ln_rows(x, g, b, tm):
    m, d = x.shape
    return pl.pallas_call(
        _ln_kernel,
        grid=(m // tm,),
        in_specs=[pl.BlockSpec((tm, d), lambda i: (i, 0)),
                  pl.BlockSpec((1, d), lambda i: (0, 0)),
                  pl.BlockSpec((1, d), lambda i: (0, 0))],
        out_specs=pl.BlockSpec((tm, d), lambda i: (i, 0)),
        out_shape=jax.ShapeDtypeStruct((m, d), F32),
        compiler_params=_cparams(1),
        name="ln_in",
    )(x, g.reshape(1, d), b.reshape(1, d))


IN_U = 0
IN_FOX_Q = IN_U + SSM_W
IN_FOX_KV = IN_FOX_Q + ATT_W
IN_F = IN_FOX_KV + 2 * ATT_W
IN_SB_Q = IN_F + LANES
IN_SB_KV = IN_SB_Q + ATT_W
IN_GATE = IN_SB_KV + 2 * ATT_W
IN_END = IN_GATE + 3 * D_MODEL


def _in_proj_kernel(x_ref, w_ref, bf_ref, u_ref, fq_ref, fkv32_ref, fkv16_ref, lf_ref,
                    sq_ref, skv32_ref, skv16_ref, g_ref, *, transposed):
    x = x_ref[...].astype(BF16)

    def seg(start, width):
        return jnp.dot(x, w_ref[:, start:start + width], preferred_element_type=F32)

    def kv(start, f32_ref, bf16_ref):
        for c in range(2):
            cols = slice(c * ATT_W, (c + 1) * ATT_W)
            z = seg(start + c * ATT_W, ATT_W)
            bf16_ref[:, cols] = z.astype(BF16)
            if transposed:
                f32_ref[0, cols, :] = z.T
            else:
                f32_ref[:, cols] = z

    u_ref[...] = seg(IN_U, SSM_W)
    fq_ref[...] = (seg(IN_FOX_Q, ATT_W) * Q_SCALE).astype(BF16)
    kv(IN_FOX_KV, fkv32_ref, fkv16_ref)
    lf_ref[...] = _log_sigmoid(seg(IN_F, LANES) + bf_ref[...])
    sq_ref[...] = (seg(IN_SB_Q, ATT_W) * Q_SCALE).astype(BF16)
    kv(IN_SB_KV, skv32_ref, skv16_ref)
    for c in range(3):
        cols = slice(c * D_MODEL, (c + 1) * D_MODEL)
        g_ref[:, cols] = _sigmoid(seg(IN_GATE + c * D_MODEL, D_MODEL)).astype(BF16)


def _in_proj(h, w, b_f, *, tm, nb):
    m, k = h.shape
    rows = lambda w_: pl.BlockSpec((tm, w_), lambda i: (i, 0))
    sds = jax.ShapeDtypeStruct
    if nb is None:
        u_spec, u_shape = rows(SSM_W), sds((m, SSM_W), F32)
        kv_spec, kv_shape = rows(2 * ATT_W), sds((m, 2 * ATT_W), F32)
    else:
        n = m // nb
        nt = n // tm
        u_spec = pl.BlockSpec((tm, SSM_W), lambda i: (i % nt, i // nt))
        u_shape = sds((n, nb * SSM_W), F32)
        kv_spec = pl.BlockSpec((1, 2 * ATT_W, tm), lambda i: (i // nt, 0, i % nt))
        kv_shape = sds((nb, 2 * ATT_W, n), F32)
    return pl.pallas_call(
        functools.partial(_in_proj_kernel, transposed=nb is not None),
        grid=(m // tm,),
        in_specs=[rows(k), pl.BlockSpec((k, IN_END), lambda i: (0, 0)),
                  pl.BlockSpec((1, LANES), lambda i: (0, 0))],
        out_specs=[u_spec, rows(ATT_W), kv_spec, rows(2 * ATT_W), rows(LANES),
                   rows(ATT_W), kv_spec, rows(2 * ATT_W), rows(3 * D_MODEL)],
        out_shape=[u_shape, sds((m, ATT_W), BF16), kv_shape, sds((m, 2 * ATT_W), BF16),
                   sds((m, LANES), F32), sds((m, ATT_W), BF16), kv_shape,
                   sds((m, 2 * ATT_W), BF16), sds((m, 3 * D_MODEL), BF16)],
        compiler_params=_cparams(1),
        name="in_proj",
    )(h, w, b_f)


def _gated_kernel(x_ref, wa_ref, wb_ref, o_ref, *, silu_a):
    x = x_ref[...].astype(BF16)
    a = jnp.dot(x, wa_ref[...], preferred_element_type=F32)
    b = jnp.dot(x, wb_ref[...], preferred_element_type=F32)
    if silu_a:
        out = a * _sigmoid(a) * b
    else:
        out = a * _sigmoid(b)
    o_ref[...] = out.astype(o_ref.dtype)


def _gated(x, wa, wb, out_dtype, *, tm, tn, silu_a, name):
    m, k = x.shape
    n = wa.shape[1]
    return pl.pallas_call(
        functools.partial(_gated_kernel, silu_a=silu_a),
        grid=(m // tm, n // tn),
        in_specs=[pl.BlockSpec((tm, k), lambda i, j: (i, 0)),
                  pl.BlockSpec((k, tn), lambda i, j: (0, j)),
                  pl.BlockSpec((k, tn), lambda i, j: (0, j))],
        out_specs=pl.BlockSpec((tm, tn), lambda i, j: (i, j)),
        out_shape=jax.ShapeDtypeStruct((m, n), out_dtype),
        compiler_params=_cparams(2),
        name=name,
    )(x, wa, wb)


def _mix_kernel(g_ref, y_ref, af_ref, as_ref, wfo_ref, wso_ref, wout_ref, h_ref, lg_ref, lb_ref,
                o_ref, *, alpha):
    d = D_MODEL
    o_fox = jnp.dot(af_ref[...], wfo_ref[...], preferred_element_type=F32)
    o_sb = jnp.dot(as_ref[...], wso_ref[...], preferred_element_type=F32)
    g = g_ref[...].astype(F32)
    mix = g[:, :d] * y_ref[...].astype(F32) + g[:, d:2 * d] * o_fox + g[:, 2 * d:] * o_sb
    out = jnp.dot(mix.astype(BF16), wout_ref[...], preferred_element_type=F32)
    o_ref[...] = _layer_norm(alpha * h_ref[...] + out, lg_ref[...], lb_ref[...])


def _mix_ln(gates, y_tb, a_fox, a_sb, w_fox_o, w_sb_o, w_out, h, ln_g, ln_b, *, nb, tm, alpha):
    m = h.shape[0]
    n = m // nb
    nt = n // tm
    d = D_MODEL
    bt = lambda w: pl.BlockSpec((tm, w), lambda b, t: (b * nt + t, 0))
    const = lambda r, c: pl.BlockSpec((r, c), lambda b, t: (0, 0))
    return pl.pallas_call(
        functools.partial(_mix_kernel, alpha=alpha),
        grid=(nb, nt),
        in_specs=[bt(3 * d),
                  pl.BlockSpec((tm, d), lambda b, t: (t, b)),
                  bt(ATT_W), bt(ATT_W),
                  const(ATT_W, d), const(ATT_W, d), const(d, d),
                  bt(d), const(1, d), const(1, d)],
        out_specs=bt(d),
        out_shape=jax.ShapeDtypeStruct((m, d), F32),
        compiler_params=_cparams(2),
        name="mix_ln",
    )(gates, y_tb, a_fox, a_sb, w_fox_o, w_sb_o, w_out, h, ln_g.reshape(1, d), ln_b.reshape(1, d))


def _res_ln_kernel(x_ref, w_ref, h_ref, lg_ref, lb_ref, o_ref, *, alpha):
    out = jnp.dot(x_ref[...], w_ref[...], preferred_element_type=F32)
    o_ref[...] = _layer_norm(alpha * h_ref[...] + out, lg_ref[...], lb_ref[...])


def _proj_res_ln(x, w, h, ln_g, ln_b, *, tm, alpha):
    m, k = x.shape
    d = w.shape[1]
    return pl.pallas_call(
        functools.partial(_res_ln_kernel, alpha=alpha),
        grid=(m // tm,),
        in_specs=[pl.BlockSpec((tm, k), lambda i: (i, 0)),
                  pl.BlockSpec((k, d), lambda i: (0, 0)),
                  pl.BlockSpec((tm, d), lambda i: (i, 0)),
                  pl.BlockSpec((1, d), lambda i: (0, 0)),
                  pl.BlockSpec((1, d), lambda i: (0, 0))],
        out_specs=pl.BlockSpec((tm, d), lambda i: (i, 0)),
        out_shape=jax.ShapeDtypeStruct((m, d), F32),
        compiler_params=_cparams(1),
        name="ffn_out_ln",
    )(x, w, h, ln_g.reshape(1, d), ln_b.reshape(1, d))


def _zoh_kernel(ar_ref, ai_ref, ldt_ref, br_ref, bi_ref, abr_ref, abi_ref, bbr_ref, bbi_ref):
    ar, ai = ar_ref[...], ai_ref[...]
    dt = jnp.exp(ldt_ref[...])
    mag = jnp.exp(dt * ar)
    abr = mag * jnp.cos(dt * ai)
    abi = mag * jnp.sin(dt * ai)
    nr, ni = abr - 1.0, abi
    den = ar * ar + ai * ai
    qr = (nr * ar + ni * ai) / den
    qi = (ni * ar - nr * ai) / den
    abr_ref[...] = abr
    abi_ref[...] = abi
    for l in range(ar.shape[0]):
        br, bi = br_ref[l], bi_ref[l]
        bbr_ref[l] = qr[l:l + 1] * br - qi[l:l + 1] * bi
        bbi_ref[l] = qr[l:l + 1] * bi + qi[l:l + 1] * br


def _ssm_kernel(u_ref, h0r_ref, h0i_ref, ar_ref, ai_ref, wb_ref, wc_ref, d_ref,
                y_ref, hr_out_ref, hi_out_ref, bur, bui, sr, si, *, nb, steps):
    @pl.when(pl.program_id(0) == 0)
    def _():
        sr[...] = h0r_ref[...]
        si[...] = h0i_ref[...]

    u = u_ref[...]
    ub = u.astype(BF16)
    n_chunk = SSM_W // LANES
    cw = STATE_W // n_chunk
    for j in range(n_chunk):
        res = jnp.dot(ub[:, j * LANES:(j + 1) * LANES], wb_ref[j], preferred_element_type=F32)
        bur[:, j * cw:(j + 1) * cw] = res[:, :cw]
        bui[:, j * cw:(j + 1) * cw] = res[:, cw:]

    scan_w = 2 * cw
    for c in range(STATE_W // scan_w):
        cs = slice(c * scan_w, (c + 1) * scan_w)
        ar = jnp.broadcast_to(ar_ref[:, cs], (nb, scan_w))
        ai = jnp.broadcast_to(ai_ref[:, cs], (nb, scan_w))

        def body(t, carry, cs=cs, ar=ar, ai=ai):
            hr, hi = carry
            rows = pl.ds(pl.multiple_of(t * nb, nb), nb)
            nr = ar * hr - ai * hi + bur[rows, cs]
            ni = ar * hi + ai * hr + bui[rows, cs]
            bur[rows, cs] = nr
            bui[rows, cs] = ni
            return nr, ni

        hr, hi = lax.fori_loop(0, steps, body, (sr[:, cs], si[:, cs]))
        sr[:, cs] = hr
        si[:, cs] = hi

    for j in range(n_chunk):
        hrj = bur[:, j * cw:(j + 1) * cw].astype(BF16)
        hij = bui[:, j * cw:(j + 1) * cw].astype(BF16)
        yj = jnp.dot(hrj, wc_ref[j, :cw, :], preferred_element_type=F32)
        yj = yj + jnp.dot(hij, wc_ref[j, cw:, :], preferred_element_type=F32)
        yj = yj + d_ref[:, j * LANES:(j + 1) * LANES] * u[:, j * LANES:(j + 1) * LANES]
        y_ref[:, j * LANES:(j + 1) * LANES] = _gelu_tanh(yj).astype(y_ref.dtype)

    hr_out_ref[...] = sr[...]
    hi_out_ref[...] = si[...]


def _ssm(u_tb, h0r, h0i, abr, abi, wb, wc, dvec, *, nb, steps_per_chunk):
    rows = u_tb.shape[0]
    n_steps = rows // nb
    n_chunks = n_steps // steps_per_chunk
    cr = steps_per_chunk * nb
    const2 = lambda r, c: pl.BlockSpec((r, c), lambda i: (0, 0))
    const3 = lambda a, r, c: pl.BlockSpec((a, r, c), lambda i: (0, 0, 0))
    return pl.pallas_call(
        functools.partial(_ssm_kernel, nb=nb, steps=steps_per_chunk),
        grid=(n_chunks,),
        in_specs=[pl.BlockSpec((cr, SSM_W), lambda i: (i, 0)),
                  const2(nb, STATE_W), const2(nb, STATE_W),
                  const2(1, STATE_W), const2(1, STATE_W),
                  const3(4, LANES, 2 * STATE_W // 4), const3(4, 2 * STATE_W // 4, LANES),
                  const2(1, SSM_W)],
        out_specs=[pl.BlockSpec((cr, SSM_W), lambda i: (i, 0)),
                   const2(nb, STATE_W), const2(nb, STATE_W)],
        out_shape=[jax.ShapeDtypeStruct((rows, SSM_W), BF16),
                   jax.ShapeDtypeStruct((nb, STATE_W), F32),
                   jax.ShapeDtypeStruct((nb, STATE_W), F32)],
        scratch_shapes=[pltpu.VMEM((cr, STATE_W), F32), pltpu.VMEM((cr, STATE_W), F32),
                        pltpu.VMEM((nb, STATE_W), F32), pltpu.VMEM((nb, STATE_W), F32)],
        compiler_params=_cparams(1),
        name="ssm",
    )(u_tb, h0r, h0i, abr, abi, wb, wc, dvec)


def _cumsum_kernel(x_ref, tri_ref, o_ref):
    rows, n = x_ref.shape
    carry = jnp.zeros((rows, LANES), F32)
    for c in range(n // LANES):
        ct = _dot01(x_ref[:, c * LANES:(c + 1) * LANES], tri_ref[...])
        o_ref[:, c * LANES:(c + 1) * LANES] = ct[:, :LANES] + carry
        carry = carry + ct[:, LANES:]


def _cumsum_lanes(x):
    return pl.pallas_call(
        _cumsum_kernel,
        out_shape=jax.ShapeDtypeStruct(x.shape, F32),
        name="cumsum_logf",
    )(x, _tri_ones("fwd_incl"))


def _attn_prologue(q_ref, v_ref, kvm_ref, qm_s, va_s, vb_s, vma_s, vmb_s, *, tq, n_real, ones_cols):
    lane512 = lax.broadcasted_iota(jnp.int32, (1, ATT_W), 1)
    even = ((lane512 >> 6) & 1) == 0

    def fill(dst_a, dst_b, rows, v):
        va = jnp.where(even, v, 0.0).astype(BF16)
        vb = jnp.where(even, 0.0, v).astype(BF16)
        if not ones_cols:
            dst_a[rows, :] = va
            dst_b[rows, :] = vb
            return
        ones = jnp.ones((v.shape[0], LANES), BF16)
        for p in range(N_HEADS // 2):
            src = slice(p * LANES, (p + 1) * LANES)
            for dst, val in ((dst_a, va), (dst_b, vb)):
                dst[rows, 2 * p * LANES:(2 * p + 1) * LANES] = val[:, src]
                dst[rows, (2 * p + 1) * LANES:(2 * p + 2) * LANES] = ones

    @pl.when(pl.program_id(1) == 0)
    def _():
        fill(vma_s, vmb_s, slice(0, PAGE), kvm_ref[:, ATT_W:].astype(F32))
        for r in range(n_real // ATT_BLK):
            rows = slice(r * ATT_BLK, (r + 1) * ATT_BLK)
            fill(va_s, vb_s, rows, v_ref[0, rows, :].astype(F32))

    first_head = lax.broadcasted_iota(jnp.int32, (tq, LANES), 1) < HEAD_DIM
    q = q_ref[0].astype(F32)
    for h in range(N_HEADS):
        qp = q[:, (h // 2) * LANES:(h // 2 + 1) * LANES]
        keep = first_head if h % 2 == 0 else jnp.logical_not(first_head)
        qm_s[h] = jnp.where(keep, qp, 0.0).astype(BF16)
    return first_head


def _fox_kernel(q_ref, k_ref, v_ref, kvm_ref, f_ref, o_ref,
                qm_s, va_s, vb_s, vma_s, vmb_s, m_s, l_s, acc_s, a_s, s_scr, p_scr,
                *, meta_only, tq, n_real):
    j = pl.program_id(1)
    first_head = _attn_prologue(q_ref, v_ref, kvm_ref, qm_s, va_s, vb_s, vma_s, vmb_s,
                                tq=tq, n_real=n_real, ones_cols=True)
    m_s[...] = jnp.full(m_s.shape, NEG, F32)
    l_s[...] = jnp.zeros(l_s.shape, F32)
    acc_s[...] = jnp.zeros(acc_s.shape, F32)
    diag_idx = 0 if meta_only else j + 1
    fref = [f_ref[0, h, pl.ds(diag_idx, 1), :][:, 0:1] for h in range(N_HEADS)]

    def block(kget, vaget, vbget, fidx, mask, tk):
        for p in range(N_HEADS // 2):
            k2 = kget(slice(p * LANES, (p + 1) * LANES))
            s2 = lax.dot_general(qm_s[2 * p:2 * p + 2].reshape(2 * tq, LANES), k2, _NT,
                                 preferred_element_type=F32)
            s_scr[2 * p, :, :tk] = s2[:tq]
            s_scr[2 * p + 1, :, :tk] = s2[tq:]

        def scores(h):
            frow = f_ref[0, h, pl.ds(fidx, 1), :][:, :tk]
            s = s_scr[h, :, :tk] - (frow - fref[h]) * LOG2E
            return s if mask is None else jnp.where(mask, s, NEG)

        for h in range(N_HEADS):
            m_old = m_s[h]
            m_new = jnp.maximum(m_old, jnp.max(scores(h), axis=1, keepdims=True))
            a_s[h] = jnp.exp2(m_old - m_new)
            m_s[h] = m_new
        for h in range(N_HEADS):
            m = m_s[h]
            m = m if tk == LANES else jnp.concatenate([m] * (tk // LANES), axis=1)
            p_scr[h, :, :tk] = jnp.exp2(scores(h) - m).astype(BF16)
        for p in range(N_HEADS // 2):
            aug = slice(2 * p * LANES, (2 * p + 2) * LANES)
            da = jnp.dot(p_scr[2 * p, :, :tk], vaget(aug), preferred_element_type=F32)
            db = jnp.dot(p_scr[2 * p + 1, :, :tk], vbget(aug), preferred_element_type=F32)
            l_s[2 * p] = a_s[2 * p] * l_s[2 * p] + da[:, LANES:]
            l_s[2 * p + 1] = a_s[2 * p + 1] * l_s[2 * p + 1] + db[:, LANES:]
            acc_s[p] = (jnp.where(first_head, a_s[2 * p], a_s[2 * p + 1]) * acc_s[p]
                        + da[:, :LANES] + db[:, :LANES])

    def meta_block(mask):
        block(lambda ln: kvm_ref[:, ln], lambda ln: vma_s[:, ln], lambda ln: vmb_s[:, ln],
              0, mask, PAGE)

    col_m = lax.broadcasted_iota(jnp.int32, (tq, PAGE), 1)
    if meta_only:
        row_m = lax.broadcasted_iota(jnp.int32, (tq, PAGE), 0)
        meta_block((col_m <= row_m) & (col_m < N_META))
    else:
        meta_block(col_m < N_META)

        def chunk(c, mask):
            rows = pl.ds(pl.multiple_of(c * ATT_BLK, ATT_BLK), ATT_BLK)
            block(lambda ln: k_ref[0, rows, ln], lambda ln: va_s[rows, ln],
                  lambda ln: vb_s[rows, ln], c + 1, mask, ATT_BLK)

        def body(c, carry):
            chunk(c, None)
            return carry

        lax.fori_loop(0, j, body, 0)
        row = lax.broadcasted_iota(jnp.int32, (tq, ATT_BLK), 0)
        col = lax.broadcasted_iota(jnp.int32, (tq, ATT_BLK), 1)
        chunk(j, col <= row)

    for p in range(N_HEADS // 2):
        l2 = jnp.where(first_head, l_s[2 * p], l_s[2 * p + 1])
        o_ref[0, :, p * LANES:(p + 1) * LANES] = (acc_s[p] / l2).astype(o_ref.dtype)


def _sb_kernel(q_ref, k_ref, v_ref, kvm_ref, tri_ref, trim_ref, o_ref,
               qm_s, va_s, vb_s, vma_s, vmb_s, r_s, acc_s, z_scr, hl_scr, p_scr,
               *, meta_only, tq, n_real):
    j = pl.program_id(1)
    _attn_prologue(q_ref, v_ref, kvm_ref, qm_s, va_s, vb_s, vma_s, vmb_s, tq=tq, n_real=n_real,
                   ones_cols=False)
    r_s[...] = jnp.zeros(r_s.shape, F32)
    acc_s[...] = jnp.zeros(acc_s.shape, F32)

    def block(kget, vaget, vbget, tri, mask, tk):
        for p in range(N_HEADS // 2):
            k2 = kget(slice(p * LANES, (p + 1) * LANES))
            z2 = lax.dot_general(qm_s[2 * p:2 * p + 2].reshape(2 * tq, LANES), k2, _NT,
                                 preferred_element_type=F32)
            z_scr[2 * p, :, :tk] = z2[:tq]
            z_scr[2 * p + 1, :, :tk] = z2[tq:]
        for h in range(N_HEADS):
            lg = _log2_sigmoid_neg(z_scr[h, :, :tk])
            if mask is not None:
                lg = jnp.where(mask, lg, 0.0)
            hi = lg.astype(BF16)
            hl_scr[h, :tq, :tk] = hi
            hl_scr[h, tq:, :tk] = (lg - hi.astype(F32)).astype(BF16)
        for h in range(N_HEADS):
            ct = (jnp.dot(hl_scr[h, :tq, :tk], tri, preferred_element_type=F32)
                  + jnp.dot(hl_scr[h, tq:, :tk], tri, preferred_element_type=F32))
            r = r_s[h]
            later = r if tk == LANES else jnp.concatenate([r] * (tk // LANES), axis=1)
            w = jnp.exp2(z_scr[h, :, :tk] + ct + later)
            if mask is not None:
                w = jnp.where(mask, w, 0.0)
            p_scr[h, :, :tk] = w.astype(BF16)
            r_s[h] = r + jnp.broadcast_to(ct[:, 0:1], (tq, LANES))
        for p in range(N_HEADS // 2):
            lanes = slice(p * LANES, (p + 1) * LANES)
            da = jnp.dot(p_scr[2 * p, :, :tk], vaget(lanes), preferred_element_type=F32)
            db = jnp.dot(p_scr[2 * p + 1, :, :tk], vbget(lanes), preferred_element_type=F32)
            acc_s[p] = acc_s[p] + da + db

    def meta_block(mask):
        block(lambda ln: kvm_ref[:, ln], lambda ln: vma_s[:, ln], lambda ln: vmb_s[:, ln],
              trim_ref[...], mask, PAGE)

    col_m = lax.broadcasted_iota(jnp.int32, (tq, PAGE), 1)
    if meta_only:
        row_m = lax.broadcasted_iota(jnp.int32, (tq, PAGE), 0)
        meta_block((col_m < row_m) & (col_m < N_META))
    else:
        def chunk(c, mask):
            rows = pl.ds(pl.multiple_of(c * ATT_BLK, ATT_BLK), ATT_BLK)
            block(lambda ln: k_ref[0, rows, ln], lambda ln: va_s[rows, ln],
                  lambda ln: vb_s[rows, ln], tri_ref[...], mask, ATT_BLK)

        row = lax.broadcasted_iota(jnp.int32, (tq, ATT_BLK), 0)
        col = lax.broadcasted_iota(jnp.int32, (tq, ATT_BLK), 1)
        chunk(j, col < row)

        def body(i, carry):
            chunk(j - 1 - i, None)
            return carry

        lax.fori_loop(0, j, body, 0)
        meta_block(col_m < N_META)

    for p in range(N_HEADS // 2):
        o_ref[0, :, p * LANES:(p + 1) * LANES] = acc_s[p].astype(o_ref.dtype)


def _prompt_attn(kind, q, kv, kv_meta, f_all=None, *, meta_only=False):
    bsz, n, _ = q.shape
    tq = PAGE if meta_only else ATT_BLK
    idx3 = lambda c: (lambda b, j: (b, 0, c))
    in_specs = [pl.BlockSpec((1, tq, ATT_W), lambda b, j: (b, j, 0)),
                pl.BlockSpec((1, n, ATT_W), idx3(0)),
                pl.BlockSpec((1, n, ATT_W), idx3(1)),
                pl.BlockSpec((PAGE, 2 * ATT_W), lambda b, j: (0, 0))]
    vw = ATT_W * (2 if kind == "fox" else 1)
    scratch = [pltpu.VMEM((N_HEADS, tq, LANES), BF16),
               pltpu.VMEM((n, vw), BF16), pltpu.VMEM((n, vw), BF16),
               pltpu.VMEM((PAGE, vw), BF16), pltpu.VMEM((PAGE, vw), BF16)]
    acc = pltpu.VMEM((N_HEADS // 2, tq, LANES), F32)
    if kind == "fox":
        kernel = _fox_kernel
        extra = [f_all]
        in_specs.append(pl.BlockSpec((1, N_HEADS) + f_all.shape[2:], lambda b, j: (b, 0, 0, 0)))
        scratch += [pltpu.VMEM((N_HEADS, tq, LANES), F32), pltpu.VMEM((N_HEADS, tq, LANES), F32), acc,
                    pltpu.VMEM((N_HEADS, tq, LANES), F32),
                    pltpu.VMEM((N_HEADS, tq, ATT_BLK), F32), pltpu.VMEM((N_HEADS, tq, ATT_BLK), BF16)]
    else:
        kernel = _sb_kernel
        extra = [_tri_ones("rev_incl", ATT_BLK)[:, :ATT_BLK], _tri_ones("rev_incl", PAGE)[:, :PAGE]]
        in_specs += [pl.BlockSpec(e.shape, lambda b, j: (0, 0)) for e in extra]
        scratch += [pltpu.VMEM((N_HEADS, tq, LANES), F32), acc,
                    pltpu.VMEM((N_HEADS, tq, ATT_BLK), F32),
                    pltpu.VMEM((N_HEADS, 2 * tq, ATT_BLK), BF16),
                    pltpu.VMEM((N_HEADS, tq, ATT_BLK), BF16)]
    return pl.pallas_call(
        functools.partial(kernel, meta_only=meta_only, tq=tq, n_real=0 if meta_only else n),
        grid=(bsz, n // tq),
        in_specs=in_specs,
        out_specs=pl.BlockSpec((1, tq, ATT_W), lambda b, j: (b, j, 0)),
        out_shape=jax.ShapeDtypeStruct((bsz, n, ATT_W), BF16),
        scratch_shapes=scratch,
        compiler_params=_cparams(2),
        name=kind + ("_meta_attn" if meta_only else "_prompt_attn"),
    )(q, kv, kv, kv_meta, *extra)


def _own_head_mask():
    sub = lax.broadcasted_iota(jnp.int32, (N_HEADS, ATT_W), 0)
    lane = lax.broadcasted_iota(jnp.int32, (N_HEADS, ATT_W), 1)
    return lax.shift_right_logical(lane, 6) == sub


def _broadcast_q_columns(q_row, qb_scr):
    r = lax.broadcasted_iota(jnp.int32, (ATT_W, ATT_W), 0)
    c = lax.broadcasted_iota(jnp.int32, (ATT_W, ATT_W), 1)
    qd = jnp.where(r == c, jnp.broadcast_to(q_row, (ATT_W, ATT_W)), 0.0).astype(BF16)
    qb_scr[...] = jnp.dot(qd, jnp.ones((ATT_W, LANES), BF16), preferred_element_type=F32)


def _page_scores(qb_scr, kt_ref):
    sub = lax.broadcasted_iota(jnp.int32, (N_HEADS, PAGE), 0)
    s = jnp.zeros((N_HEADS, PAGE), F32)
    for h in range(N_HEADS):
        t = qb_scr[h * HEAD_DIM:(h + 1) * HEAD_DIM, :] * kt_ref[h]
        s = jnp.where(sub == h, jnp.broadcast_to(jnp.sum(t, axis=0, keepdims=True), (N_HEADS, PAGE)), s)
    return s


def _add_page_values(acc_scr, w, vt_ref):
    for h in range(N_HEADS):
        rows = slice(h * HEAD_DIM, (h + 1) * HEAD_DIM)
        acc_scr[rows, :] = acc_scr[rows, :] + vt_ref[h] * jnp.broadcast_to(w[h:h + 1, :], (HEAD_DIM, PAGE))


def _lane_sums_as_row(acc_scr):
    ones = jnp.ones((SUBLANES, LANES), BF16)
    out = None
    for part in _split_bf16(acc_scr[...], 3):
        d = lax.dot_general(ones, part, _NT, preferred_element_type=F32)
        out = d if out is None else out + d
    return out[0:1, :]


def _sample_fox_kernel(pt_ref, q_ref, kown_ref, vown_ref, lfown_ref, tri_ref, *refs):
    del pt_ref
    kt, vt, lf = refs[:N_PAGES], refs[N_PAGES:2 * N_PAGES], refs[2 * N_PAGES:3 * N_PAGES]
    o_ref, qb_scr, acc_scr = refs[3 * N_PAGES:]
    q_row = q_ref[0].astype(F32)
    _broadcast_q_columns(q_row, qb_scr)
    own_head = _own_head_mask()
    carry = jnp.broadcast_to(lfown_ref[0], (N_HEADS, PAGE))
    scores = [None] * N_PAGES
    for p in reversed(range(N_PAGES)):
        ct = _dot01(lf[p][...], tri_ref[...])
        scores[p] = _page_scores(qb_scr, kt[p]) + (ct[:, :PAGE] + carry) * LOG2E
        carry = carry + ct[:, PAGE:]
    q_bd = jnp.where(own_head, jnp.broadcast_to(q_row, (N_HEADS, ATT_W)), 0.0)
    s_own = jnp.sum(q_bd * kown_ref[0], axis=1, keepdims=True)
    m = s_own
    for p in range(N_PAGES):
        m = jnp.maximum(m, jnp.max(scores[p], axis=1, keepdims=True))
    e_own = jnp.exp2(s_own - m)
    l = e_own
    acc_scr[...] = jnp.zeros(acc_scr.shape, F32)
    for p in range(N_PAGES):
        e = jnp.exp2(scores[p] - m)
        l = l + jnp.sum(e, axis=1, keepdims=True)
        _add_page_values(acc_scr, e, vt[p])
    own_row = jnp.sum(jnp.where(own_head, e_own * vown_ref[0], 0.0), axis=0, keepdims=True)
    l_row = jnp.sum(jnp.where(own_head, jnp.broadcast_to(l, (N_HEADS, ATT_W)), 0.0), axis=0, keepdims=True)
    o_ref[0] = ((_lane_sums_as_row(acc_scr) + own_row) / l_row).astype(o_ref.dtype)


def _sample_sb_kernel(pt_ref, q_ref, tri_ref, *refs):
    del pt_ref
    kt, vt = refs[:N_PAGES], refs[N_PAGES:2 * N_PAGES]
    o_ref, qb_scr, acc_scr = refs[2 * N_PAGES:]
    _broadcast_q_columns(q_ref[0].astype(F32), qb_scr)
    acc_scr[...] = jnp.zeros(acc_scr.shape, F32)
    carry = jnp.zeros((N_HEADS, PAGE), F32)
    for p in reversed(range(N_PAGES)):
        z = _page_scores(qb_scr, kt[p])
        ct = _dot01(_log2_sigmoid_neg(z), tri_ref[...])
        _add_page_values(acc_scr, jnp.exp2(z + ct[:, :PAGE] + carry), vt[p])
        carry = carry + ct[:, PAGE:]
    o_ref[0] = _lane_sums_as_row(acc_scr).astype(o_ref.dtype)


def _sample_attn(kind, layer, page_table, q, cache_kt, cache_vt, *, k_own=None, v_own=None, lf_own=None,
                 cache_lf_t=None):
    db = q.shape[0]
    row3 = pl.BlockSpec((1, 1, ATT_W), lambda b, pt: (b, 0, 0))
    tri_spec = pl.BlockSpec((LANES, 2 * LANES), lambda b, pt: (0, 0))

    def pages(block):
        zeros = (0,) * (len(block) - 2)
        return [pl.BlockSpec(block, lambda b, pt, p=p: (layer, pt[b * N_PAGES + p]) + zeros)
                for p in range(N_PAGES)]

    kv_block = (None, None, N_HEADS, HEAD_DIM, PAGE)
    scratch = [pltpu.VMEM((ATT_W, LANES), F32), pltpu.VMEM((ATT_W, PAGE), F32)]
    if kind == "fox":
        kernel = _sample_fox_kernel
        in_specs = ([row3, row3, row3, pl.BlockSpec((1, N_HEADS, 1), lambda b, pt: (b, 0, 0)), tri_spec]
                    + pages(kv_block) + pages(kv_block) + pages((None, None, N_HEADS, PAGE)))
        args = ([q, k_own, v_own, lf_own, _tri_ones("rev_excl")]
                + [cache_kt] * N_PAGES + [cache_vt] * N_PAGES + [cache_lf_t] * N_PAGES)
    else:
        kernel = _sample_sb_kernel
        in_specs = [row3, tri_spec] + pages(kv_block) + pages(kv_block)
        args = [q, _tri_ones("rev_incl")] + [cache_kt] * N_PAGES + [cache_vt] * N_PAGES
    return pl.pallas_call(
        kernel,
        grid_spec=pltpu.PrefetchScalarGridSpec(
            num_scalar_prefetch=1, grid=(db,), in_specs=in_specs, out_specs=row3,
            scratch_shapes=scratch),
        out_shape=jax.ShapeDtypeStruct((db, 1, ATT_W), BF16),
        compiler_params=_cparams(1),
        name="sample_" + kind,
    )(page_table.reshape(-1), *args)


def _block_diag_ssm_weights(bb_re_t, bb_im_t, c_re, c_im):
    eye = jnp.eye(8, dtype=F32)

    def b_blocks(bt):
        b = bt.reshape(SSM_GROUP, 4, 8, SSM_STATE)
        return jnp.einsum("cjgp,gh->jgchp", b, eye).reshape(4, LANES, STATE_W // 4)

    def c_blocks(c):
        c4 = c.reshape(4, 8, SSM_GROUP, SSM_STATE)
        return jnp.einsum("jgcp,gh->jgphc", c4, eye).reshape(4, STATE_W // 4, LANES)

    wb = jnp.concatenate([b_blocks(bb_re_t), b_blocks(bb_im_t)], axis=2).astype(BF16)
    wc = jnp.concatenate([c_blocks(c_re), -c_blocks(c_im)], axis=1).astype(BF16)
    return wb, wc


def kernel(x_prompt, x_sample, cache_fox_k, cache_fox_v, cache_fox_logf, cache_sb_k, cache_sb_v,
           state_ssm_re, state_ssm_im, page_table, meta_tokens, ln_in_g, ln_in_b, w_in, b_forget,
           ssm_a_re, ssm_a_im, ssm_log_dt, ssm_b_re, ssm_b_im, ssm_c_re, ssm_c_im, ssm_d, w_ssm_glu,
           w_fox_o, w_sb_o, w_out, ln1_g, ln1_b, w_ffn_in, w_ffn_out, ln2_g, ln2_b):
    bsz, seq, d = x_prompt.shape
    db = x_sample.shape[0]
    depth = w_in.shape[0]
    d_ff = w_ffn_out.shape[1]
    n_small = db + N_META
    alpha = (2 * depth) ** 0.25
    n_chunks = seq // ATT_BLK

    off_f = IN_F
    off_sq = off_f + N_HEADS

    flat = lambda a: a.reshape(depth, STATE_W)
    ldt_rep = jnp.repeat(ssm_log_dt, SSM_STATE, axis=1)
    b_t = lambda a: a.reshape(depth, STATE_W, SSM_GROUP).transpose(0, 2, 1)
    abr, abi, bbr_t, bbi_t = pl.pallas_call(
        _zoh_kernel,
        out_shape=[jax.ShapeDtypeStruct((depth, STATE_W), F32)] * 2
        + [jax.ShapeDtypeStruct((depth, SSM_GROUP, STATE_W), F32)] * 2,
        name="zoh",
    )(flat(ssm_a_re), flat(ssm_a_im), ldt_rep, b_t(ssm_b_re), b_t(ssm_b_im))

    w_in16 = w_in.astype(BF16)
    w_glu16 = w_ssm_glu.astype(BF16)
    w_fox_o16 = w_fox_o.astype(BF16)
    w_sb_o16 = w_sb_o.astype(BF16)
    w_out16 = w_out.astype(BF16)
    w_ffn_in16 = w_ffn_in.astype(BF16)
    w_ffn_out16 = w_ffn_out.astype(BF16)

    pos_minor = lambda c: c.transpose(0, 1, 3, 4, 2)
    cache_fox_kt, cache_fox_vt = pos_minor(cache_fox_k), pos_minor(cache_fox_v)
    cache_sb_kt, cache_sb_vt = pos_minor(cache_sb_k), pos_minor(cache_sb_v)
    cache_lf_t = cache_fox_logf.astype(F32).transpose(0, 1, 3, 2)

    tm_big = 512
    h_big = _ln_rows(x_prompt.reshape(bsz * seq, d), ln_in_g, ln_in_b, tm_big)
    x_small = jnp.concatenate([x_sample.reshape(db, d), meta_tokens.astype(x_prompt.dtype)], axis=0)
    h_small = _ln_rows(x_small, ln_in_g, ln_in_b, n_small)

    outs = {k: [] for k in ("pfk", "pfv", "plf", "psk", "psv", "phr", "phi",
                            "sfk", "sfv", "slf", "ssk", "ssv", "shr", "shi")}

    for l in range(depth):
        pad_f = ((0, 0), (0, LANES - N_HEADS))
        wl = jnp.concatenate([w_in16[l][:, :off_f], jnp.pad(w_in16[l][:, off_f:off_sq], pad_f),
                              w_in16[l][:, off_sq:]], axis=1)
        b_f = jnp.pad(b_forget[l].reshape(1, N_HEADS), pad_f)
        wb, wc = _block_diag_ssm_weights(bbr_t[l], bbi_t[l], ssm_c_re[l], ssm_c_im[l])
        dvec = ssm_d[l].reshape(1, SSM_W)
        ar_l, ai_l = abr[l:l + 1], abi[l:l + 1]

        def in_proj(h, tm, nb):
            u, fq, fkv32, fkv16, lf, sq, skv32, skv16, gates = _in_proj(h, wl, b_f, tm=tm, nb=nb)
            return u, fq, fkv32, fkv16, lf[:, :N_HEADS], sq, skv32, skv16, gates

        u_s, fq_s, fkv32_s, fkv16_s, lf_s, sq_s, skv32_s, skv16_s, gates_s = in_proj(h_small, n_small, None)

        y_smp, shr, shi = _ssm(u_s[:db], state_ssm_re[l].reshape(db, STATE_W),
                               state_ssm_im[l].reshape(db, STATE_W), ar_l, ai_l, wb, wc, dvec,
                               nb=db, steps_per_chunk=1)
        zeros8 = jnp.zeros((SUBLANES, STATE_W), F32)
        y_meta, mhr, mhi = _ssm(jnp.repeat(u_s[db:], SUBLANES, axis=0), zeros8, zeros8,
                                ar_l, ai_l, wb, wc, dvec, nb=SUBLANES, steps_per_chunk=N_META)
        y_small = jnp.concatenate([y_smp, y_meta[::SUBLANES]], axis=0)

        u_b, fq_b, fkv32_b, fkv16_b, lf_b, sq_b, skv32_b, skv16_b, gates_b = in_proj(h_big, ATT_BLK, bsz)

        lf_meta_t = jnp.pad(lf_s[db:].T, ((0, 0), (0, ATT_BLK - N_META)))
        lf_real_t = lf_b.reshape(bsz, seq, N_HEADS).transpose(0, 2, 1)
        lf_all = jnp.concatenate(
            [jnp.broadcast_to(lf_meta_t[None], (bsz, N_HEADS, ATT_BLK)), lf_real_t], axis=2)
        f_all = _cumsum_lanes(lf_all.reshape(bsz * N_HEADS, ATT_BLK + seq))
        f_all = f_all.reshape(bsz, N_HEADS, n_chunks + 1, ATT_BLK)

        pad_meta = lambda a: jnp.pad(a[db:], ((0, PAGE - N_META), (0, 0)))
        fkv_meta = pad_meta(fkv16_s)
        skv_meta = pad_meta(skv16_s)

        a_fox_meta = _prompt_attn("fox", pad_meta(fq_s)[None], fkv_meta[None], fkv_meta,
                                  f_all[:1], meta_only=True)[0, :N_META]
        a_sb_meta = _prompt_attn("sb", pad_meta(sq_s)[None], skv_meta[None], skv_meta,
                                 meta_only=True)[0, :N_META]

        row3 = lambda a: a.reshape(db, 1, ATT_W)
        a_fox_smp = _sample_attn(
            "fox", l, page_table, row3(fq_s[:db]), cache_fox_kt, cache_fox_vt,
            k_own=row3(fkv32_s[:db, :ATT_W]), v_own=row3(fkv32_s[:db, ATT_W:]),
            lf_own=lf_s[:db].reshape(db, N_HEADS, 1), cache_lf_t=cache_lf_t)
        a_sb_smp = _sample_attn("sb", l, page_table, row3(sq_s[:db]), cache_sb_kt, cache_sb_vt)
        a_fox_small = jnp.concatenate([a_fox_smp.reshape(db, ATT_W), a_fox_meta], axis=0)
        a_sb_small = jnp.concatenate([a_sb_smp.reshape(db, ATT_W), a_sb_meta], axis=0)

        u_b = u_b.reshape(seq * bsz, SSM_W)
        y_big, phr, phi = _ssm(u_b, mhr, mhi, ar_l, ai_l, wb, wc, dvec, nb=bsz, steps_per_chunk=64)
        a_fox_big = _prompt_attn("fox", fq_b.reshape(bsz, seq, ATT_W),
                                 fkv16_b.reshape(bsz, seq, 2 * ATT_W), fkv_meta, f_all)
        a_sb_big = _prompt_attn("sb", sq_b.reshape(bsz, seq, ATT_W),
                                skv16_b.reshape(bsz, seq, 2 * ATT_W), skv_meta)

        def tail(h, y, a_fox, a_sb, gates, nb, tm):
            m = h.shape[0]
            glu = _gated(y, w_glu16[l][:, :d], w_glu16[l][:, d:], BF16, tm=tm, tn=d, silu_a=False,
                         name="ssm_glu")
            h1 = _mix_ln(gates, glu.reshape(m // nb, nb * d), a_fox, a_sb, w_fox_o16[l], w_sb_o16[l],
                         w_out16[l], h, ln1_g[l], ln1_b[l], nb=nb, tm=tm, alpha=alpha)
            act = _gated(h1, w_ffn_in16[l][:, :d_ff], w_ffn_in16[l][:, d_ff:], BF16, tm=tm,
                         tn=d_ff // 2, silu_a=True, name="ffn_in")
            return _proj_res_ln(act, w_ffn_out16[l], h1, ln2_g[l], ln2_b[l], tm=tm, alpha=alpha)

        h_small = tail(h_small, y_small, a_fox_small, a_sb_small, gates_s, 1, n_small)
        h_big = tail(h_big, y_big, a_fox_big.reshape(bsz * seq, ATT_W),
                     a_sb_big.reshape(bsz * seq, ATT_W), gates_b, bsz, tm_big)

        def with_meta(real, meta):
            w = real.shape[1]
            return jnp.concatenate(
                [jnp.broadcast_to(meta[None], (bsz, N_META, w)), real.reshape(bsz, seq, w)], axis=1)

        def with_meta_t(real_t, meta):
            full = jnp.concatenate(
                [jnp.broadcast_to(meta.T[None], (bsz, ATT_W, N_META)), real_t], axis=2)
            return full.reshape(bsz, N_HEADS, HEAD_DIM, N_META + seq).transpose(0, 3, 1, 2)

        heads = lambda a: a.reshape(a.shape[:-1] + (N_HEADS, HEAD_DIM))
        outs["pfk"].append(with_meta_t(fkv32_b[:, :ATT_W], fkv32_s[db:, :ATT_W]))
        outs["pfv"].append(with_meta_t(fkv32_b[:, ATT_W:], fkv32_s[db:, ATT_W:]))
        outs["plf"].append(with_meta(lf_b, lf_s[db:]))
        outs["psk"].append(with_meta_t(skv32_b[:, :ATT_W], skv32_s[db:, :ATT_W]))
        outs["psv"].append(with_meta_t(skv32_b[:, ATT_W:], skv32_s[db:, ATT_W:]))
        outs["phr"].append(phr.reshape(bsz, N_GROUPS, SSM_STATE))
        outs["phi"].append(phi.reshape(bsz, N_GROUPS, SSM_STATE))
        outs["sfk"].append(heads(fkv32_s[:db, :ATT_W].reshape(db, 1, ATT_W)))
        outs["sfv"].append(heads(fkv32_s[:db, ATT_W:].reshape(db, 1, ATT_W)))
        outs["slf"].append(lf_s[:db].reshape(db, 1, N_HEADS))
        outs["ssk"].append(heads(skv32_s[:db, :ATT_W].reshape(db, 1, ATT_W)))
        outs["ssv"].append(heads(skv32_s[:db, ATT_W:].reshape(db, 1, ATT_W)))
        outs["shr"].append(shr.reshape(db, N_GROUPS, SSM_STATE))
        outs["shi"].append(shi.reshape(db, N_GROUPS, SSM_STATE))

    st = {k: jnp.stack(v) for k, v in outs.items()}
    return (h_big.reshape(bsz, seq, d), h_small[:db].reshape(db, 1, d),
            st["pfk"], st["pfv"], st["plf"], st["psk"], st["psv"], st["phr"], st["phi"],
            st["sfk"], st["sfv"], st["slf"], st["ssk"], st["ssv"], st["shr"], st["shi"])
```

```python
import functools
import math

import jax
import jax.numpy as jnp
from jax import lax
from jax.experimental import pallas as pl
from jax.experimental.pallas import tpu as pltpu

F32 = jnp.float32
BF16 = jnp.bfloat16

D_MODEL = 1024
N_META = 16
HEAD_DIM = 64
N_HEADS = 8
ATT_W = N_HEADS * HEAD_DIM
SSM_W = 512
SSM_GROUP = 16
N_GROUPS = 32
SSM_STATE = 64
STATE_W = N_GROUPS * SSM_STATE
PAGE = 128
N_PAGES = 16
ATT_BLK = 256
LN_EPS = 1e-5
NEG = -1e30
LOG2E = math.log2(math.e)
Q_SCALE = LOG2E * HEAD_DIM ** -0.5

LANES = 128
SUBLANES = 8
VMEM_LIMIT = 48 * 1024 * 1024

_NT = (((1,), (1,)), ((), ()))


def _cparams(n_axes):
    return pltpu.CompilerParams(
        dimension_semantics=("arbitrary",) * n_axes, vmem_limit_bytes=VMEM_LIMIT)


def _sigmoid(x):
    return 1.0 / (1.0 + jnp.exp(-x))


def _log_sigmoid(x):
    return jnp.minimum(x, 0.0) - jnp.log(1.0 + jnp.exp(-jnp.abs(x)))


def _log2_sigmoid_neg(z2):
    nz = -z2
    return jnp.minimum(nz, 0.0) - jnp.log2(1.0 + jnp.exp2(jnp.minimum(z2, nz)))


def _gelu_tanh(x):
    return 0.5 * x * (1.0 + jnp.tanh(math.sqrt(2.0 / math.pi) * (x + 0.044715 * (x * x * x))))


def _layer_norm(x, g, b):
    mu = jnp.mean(x, axis=-1, keepdims=True)
    xc = x - mu
    var = jnp.mean(xc * xc, axis=-1, keepdims=True)
    return xc * lax.rsqrt(var + LN_EPS) * g + b


def _split_bf16(x, n_terms):
    terms = []
    for _ in range(n_terms - 1):
        t = x.astype(BF16)
        terms.append(t)
        x = x - t.astype(F32)
    terms.append(x.astype(BF16))
    return terms


def _dot01(x, m01, n_terms=3):
    rows = x.shape[0]
    out = jnp.dot(jnp.concatenate(_split_bf16(x, n_terms), axis=0), m01, preferred_element_type=F32)
    acc = out[:rows]
    for t in range(1, n_terms):
        acc = acc + out[t * rows:(t + 1) * rows]
    return acc


def _tri_ones(kind, n=LANES):
    i = jnp.arange(n)[:, None]
    j = jnp.arange(n)[None, :]
    tri = {"fwd_incl": i <= j, "rev_incl": i >= j, "rev_excl": i > j}[kind]
    return jnp.concatenate([tri.astype(BF16), jnp.ones((n, LANES), BF16)], axis=1)


def _ln_kernel(x_ref, g_ref, b_ref, o_ref):
    o_ref[...] = _layer_norm(x_ref[...], g_ref[...], b_ref[...])


def _ln_rows(x, g, b, tm):
    m, d = x.shape
    return pl.pallas_call(
        _ln_kernel,
        grid=(m // tm,),
        in_specs=[pl.BlockSpec((tm, d), lambda i: (i, 0)),
                  pl.BlockSpec((1, d), lambda i: (0, 0)),
                  pl.BlockSpec((1, d), lambda i: (0, 0))],
        out_specs=pl.BlockSpec((tm, d), lambda i: (i, 0)),
        out_shape=jax.ShapeDtypeStruct((m, d), F32),
        compiler_params=_cparams(1),
        name="ln_in",
    )(x, g.reshape(1, d), b.reshape(1, d))


IN_U = 0
IN_FOX_Q = IN_U + SSM_W
IN_FOX_KV = IN_FOX_Q + ATT_W
IN_F = IN_FOX_KV + 2 * ATT_W
IN_SB_Q = IN_F + LANES
IN_SB_KV = IN_SB_Q + ATT_W
IN_GATE = IN_SB_KV + 2 * ATT_W
IN_END = IN_GATE + 3 * D_MODEL


def _in_proj_kernel(x_ref, w_ref, bf_ref, u_ref, fq_ref, fkv32_ref, fkv16_ref, lf_ref,
                    sq_ref, skv32_ref, skv16_ref, g_ref, *, transposed):
    x = x_ref[...].astype(BF16)

    def seg(start, width):
        return jnp.dot(x, w_ref[:, start:start + width], preferred_element_type=F32)

    def kv(start, f32_ref, bf16_ref):
        for c in range(2):
            cols = slice(c * ATT_W, (c + 1) * ATT_W)
            z = seg(start + c * ATT_W, ATT_W)
            bf16_ref[:, cols] = z.astype(BF16)
            if transposed:
                f32_ref[0, cols, :] = z.T
            else:
                f32_ref[:, cols] = z

    u_ref[...] = seg(IN_U, SSM_W)
    fq_ref[...] = (seg(IN_FOX_Q, ATT_W) * Q_SCALE).astype(BF16)
    kv(IN_FOX_KV, fkv32_ref, fkv16_ref)
    lf_ref[...] = _log_sigmoid(seg(IN_F, LANES) + bf_ref[...])
    sq_ref[...] = (seg(IN_SB_Q, ATT_W) * Q_SCALE).astype(BF16)
    kv(IN_SB_KV, skv32_ref, skv16_ref)
    for c in range(3):
        cols = slice(c * D_MODEL, (c + 1) * D_MODEL)
        g_ref[:, cols] = _sigmoid(seg(IN_GATE + c * D_MODEL, D_MODEL)).astype(BF16)


def _in_proj(h, w, b_f, *, tm, nb):
    m, k = h.shape
    rows = lambda w_: pl.BlockSpec((tm, w_), lambda i: (i, 0))
    sds = jax.ShapeDtypeStruct
    if nb is None:
        u_spec, u_shape = rows(SSM_W), sds((m, SSM_W), F32)
        kv_spec, kv_shape = rows(2 * ATT_W), sds((m, 2 * ATT_W), F32)
    else:
        n = m // nb
        nt = n // tm
        u_spec = pl.BlockSpec((tm, SSM_W), lambda i: (i % nt, i // nt))
        u_shape = sds((n, nb * SSM_W), F32)
        kv_spec = pl.BlockSpec((1, 2 * ATT_W, tm), lambda i: (i // nt, 0, i % nt))
        kv_shape = sds((nb, 2 * ATT_W, n), F32)
    return pl.pallas_call(
        functools.partial(_in_proj_kernel, transposed=nb is not None),
        grid=(m // tm,),
        in_specs=[rows(k), pl.BlockSpec((k, IN_END), lambda i: (0, 0)),
                  pl.BlockSpec((1, LANES), lambda i: (0, 0))],
        out_specs=[u_spec, rows(ATT_W), kv_spec, rows(2 * ATT_W), rows(LANES),
                   rows(ATT_W), kv_spec, rows(2 * ATT_W), rows(3 * D_MODEL)],
        out_shape=[u_shape, sds((m, ATT_W), BF16), kv_shape, sds((m, 2 * ATT_W), BF16),
                   sds((m, LANES), F32), sds((m, ATT_W), BF16), kv_shape,
                   sds((m, 2 * ATT_W), BF16), sds((m, 3 * D_MODEL), BF16)],
        compiler_params=_cparams(1),
        name="in_proj",
    )(h, w, b_f)


def _gated_kernel(x_ref, wa_ref, wb_ref, o_ref, *, silu_a):
    x = x_ref[...].astype(BF16)
    a = jnp.dot(x, wa_ref[...], preferred_element_type=F32)
    b = jnp.dot(x, wb_ref[...], preferred_element_type=F32)
    if silu_a:
        out = a * _sigmoid(a) * b
    else:
        out = a * _sigmoid(b)
    o_ref[...] = out.astype(o_ref.dtype)


def _gated(x, wa, wb, out_dtype, *, tm, tn, silu_a, name):
    m, k = x.shape
    n = wa.shape[1]
    return pl.pallas_call(
        functools.partial(_gated_kernel, silu_a=silu_a),
        grid=(m // tm, n // tn),
        in_specs=[pl.BlockSpec((tm, k), lambda i, j: (i, 0)),
                  pl.BlockSpec((k, tn), lambda i, j: (0, j)),
                  pl.BlockSpec((k, tn), lambda i, j: (0, j))],
        out_specs=pl.BlockSpec((tm, tn), lambda i, j: (i, j)),
        out_shape=jax.ShapeDtypeStruct((m, n), out_dtype),
        compiler_params=_cparams(2),
        name=name,
    )(x, wa, wb)


def _mix_kernel(g_ref, y_ref, af_ref, as_ref, wfo_ref, wso_ref, wout_ref, h_ref, lg_ref, lb_ref,
                o_ref, *, alpha):
    d = D_MODEL
    o_fox = jnp.dot(af_ref[...], wfo_ref[...], preferred_element_type=F32)
    o_sb = jnp.dot(as_ref[...], wso_ref[...], preferred_element_type=F32)
    g = g_ref[...].astype(F32)
    mix = g[:, :d] * y_ref[...].astype(F32) + g[:, d:2 * d] * o_fox + g[:, 2 * d:] * o_sb
    out = jnp.dot(mix.astype(BF16), wout_ref[...], preferred_element_type=F32)
    o_ref[...] = _layer_norm(alpha * h_ref[...] + out, lg_ref[...], lb_ref[...])


def _mix_ln(gates, y_tb, a_fox, a_sb, w_fox_o, w_sb_o, w_out, h, ln_g, ln_b, *, nb, tm, alpha):
    m = h.shape[0]
    n = m // nb
    nt = n // tm
    d = D_MODEL
    bt = lambda w: pl.BlockSpec((tm, w), lambda b, t: (b * nt + t, 0))
    const = lambda r, c: pl.BlockSpec((r, c), lambda b, t: (0, 0))
    return pl.pallas_call(
        functools.partial(_mix_kernel, alpha=alpha),
        grid=(nb, nt),
        in_specs=[bt(3 * d),
                  pl.BlockSpec((tm, d), lambda b, t: (t, b)),
                  bt(ATT_W), bt(ATT_W),
                  const(ATT_W, d), const(ATT_W, d), const(d, d),
                  bt(d), const(1, d), const(1, d)],
        out_specs=bt(d),
        out_shape=jax.ShapeDtypeStruct((m, d), F32),
        compiler_params=_cparams(2),
        name="mix_ln",
    )(gates, y_tb, a_fox, a_sb, w_fox_o, w_sb_o, w_out, h, ln_g.reshape(1, d), ln_b.reshape(1, d))


def _res_ln_kernel(x_ref, w_ref, h_ref, lg_ref, lb_ref, o_ref, *, alpha):
    out = jnp.dot(x_ref[...], w_ref[...], preferred_element_type=F32)
    o_ref[...] = _layer_norm(alpha * h_ref[...] + out, lg_ref[...], lb_ref[...])


def _proj_res_ln(x, w, h, ln_g, ln_b, *, tm, alpha):
    m, k = x.shape
    d = w.shape[1]
    return pl.pallas_call(
        functools.partial(_res_ln_kernel, alpha=alpha),
        grid=(m // tm,),
        in_specs=[pl.BlockSpec((tm, k), lambda i: (i, 0)),
                  pl.BlockSpec((k, d), lambda i: (0, 0)),
                  pl.BlockSpec((tm, d), lambda i: (i, 0)),
                  pl.BlockSpec((1, d), lambda i: (0, 0)),
                  pl.BlockSpec((1, d), lambda i: (0, 0))],
        out_specs=pl.BlockSpec((tm, d), lambda i: (i, 0)),
        out_shape=jax.ShapeDtypeStruct((m, d), F32),
        compiler_params=_cparams(1),
        name="ffn_out_ln",
    )(x, w, h, ln_g.reshape(1, d), ln_b.reshape(1, d))


def _zoh_kernel(ar_ref, ai_ref, ldt_ref, br_ref, bi_ref, abr_ref, abi_ref, bbr_ref, bbi_ref):
    ar, ai = ar_ref[...], ai_ref[...]
    dt = jnp.exp(ldt_ref[...])
    mag = jnp.exp(dt * ar)
    abr = mag * jnp.cos(dt * ai)
    abi = mag * jnp.sin(dt * ai)
    nr, ni = abr - 1.0, abi
    den = ar * ar + ai * ai
    qr = (nr * ar + ni * ai) / den
    qi = (ni * ar - nr * ai) / den
    abr_ref[...] = abr
    abi_ref[...] = abi
    for l in range(ar.shape[0]):
        br, bi = br_ref[l], bi_ref[l]
        bbr_ref[l] = qr[l:l + 1] * br - qi[l:l + 1] * bi
        bbi_ref[l] = qr[l:l + 1] * bi + qi[l:l + 1] * br


def _ssm_kernel(u_ref, h0r_ref, h0i_ref, ar_ref, ai_ref, wb_ref, wc_ref, d_ref, wga_ref, wgb_ref,
                g_ref, hr_out_ref, hi_out_ref, bur, bui, sr, si, y_ref, *, nb, steps):
    @pl.when(pl.program_id(0) == 0)
    def _():
        sr[...] = h0r_ref[...]
        si[...] = h0i_ref[...]

    u = u_ref[...]
    ub = u.astype(BF16)
    n_chunk = SSM_W // LANES
    cw = STATE_W // n_chunk
    for j in range(n_chunk):
        res = jnp.dot(ub[:, j * LANES:(j + 1) * LANES], wb_ref[j], preferred_element_type=F32)
        bur[:, j * cw:(j + 1) * cw] = res[:, :cw]
        bui[:, j * cw:(j + 1) * cw] = res[:, cw:]

    scan_w = 2 * cw
    for c in range(STATE_W // scan_w):
        cs = slice(c * scan_w, (c + 1) * scan_w)
        ar = jnp.broadcast_to(ar_ref[:, cs], (nb, scan_w))
        ai = jnp.broadcast_to(ai_ref[:, cs], (nb, scan_w))

        def body(t, carry, cs=cs, ar=ar, ai=ai):
            hr, hi = carry
            rows = pl.ds(pl.multiple_of(t * nb, nb), nb)
            nr = ar * hr - ai * hi + bur[rows, cs]
            ni = ar * hi + ai * hr + bui[rows, cs]
            bur[rows, cs] = nr
            bui[rows, cs] = ni
            return nr, ni

        hr, hi = lax.fori_loop(0, steps, body, (sr[:, cs], si[:, cs]))
        sr[:, cs] = hr
        si[:, cs] = hi

    for j in range(n_chunk):
        hrj = bur[:, j * cw:(j + 1) * cw].astype(BF16)
        hij = bui[:, j * cw:(j + 1) * cw].astype(BF16)
        yj = jnp.dot(hrj, wc_ref[j, :cw, :], preferred_element_type=F32)
        yj = yj + jnp.dot(hij, wc_ref[j, cw:, :], preferred_element_type=F32)
        yj = yj + d_ref[:, j * LANES:(j + 1) * LANES] * u[:, j * LANES:(j + 1) * LANES]
        y_ref[:, j * LANES:(j + 1) * LANES] = _gelu_tanh(yj).astype(y_ref.dtype)

    y = y_ref[...]
    ga = jnp.dot(y, wga_ref[...], preferred_element_type=F32)
    gb = jnp.dot(y, wgb_ref[...], preferred_element_type=F32)
    g_ref[...] = (ga * _sigmoid(gb)).astype(g_ref.dtype)
    hr_out_ref[...] = sr[...]
    hi_out_ref[...] = si[...]


def _ssm(u_tb, h0r, h0i, abr, abi, wb, wc, dvec, wg, *, nb, steps_per_chunk):
    rows = u_tb.shape[0]
    n_steps = rows // nb
    n_chunks = n_steps // steps_per_chunk
    cr = steps_per_chunk * nb
    const2 = lambda r, c: pl.BlockSpec((r, c), lambda i: (0, 0))
    const3 = lambda a, r, c: pl.BlockSpec((a, r, c), lambda i: (0, 0, 0))
    return pl.pallas_call(
        functools.partial(_ssm_kernel, nb=nb, steps=steps_per_chunk),
        grid=(n_chunks,),
        in_specs=[pl.BlockSpec((cr, SSM_W), lambda i: (i, 0)),
                  const2(nb, STATE_W), const2(nb, STATE_W),
                  const2(1, STATE_W), const2(1, STATE_W),
                  const3(4, LANES, 2 * STATE_W // 4), const3(4, 2 * STATE_W // 4, LANES),
                  const2(1, SSM_W), const2(SSM_W, D_MODEL), const2(SSM_W, D_MODEL)],
        out_specs=[pl.BlockSpec((cr, D_MODEL), lambda i: (i, 0)),
                   const2(nb, STATE_W), const2(nb, STATE_W)],
        out_shape=[jax.ShapeDtypeStruct((rows, D_MODEL), BF16),
                   jax.ShapeDtypeStruct((nb, STATE_W), F32),
                   jax.ShapeDtypeStruct((nb, STATE_W), F32)],
        scratch_shapes=[pltpu.VMEM((cr, STATE_W), F32), pltpu.VMEM((cr, STATE_W), F32),
                        pltpu.VMEM((nb, STATE_W), F32), pltpu.VMEM((nb, STATE_W), F32),
                        pltpu.VMEM((cr, SSM_W), BF16)],
        compiler_params=_cparams(1),
        name="ssm_glu",
    )(u_tb, h0r, h0i, abr, abi, wb, wc, dvec, *wg)


def _cumsum_kernel(x_ref, tri_ref, o_ref):
    rows, n = x_ref.shape
    carry = jnp.zeros((rows, LANES), F32)
    for c in range(n // LANES):
        ct = _dot01(x_ref[:, c * LANES:(c + 1) * LANES], tri_ref[...])
        o_ref[:, c * LANES:(c + 1) * LANES] = ct[:, :LANES] + carry
        carry = carry + ct[:, LANES:]


def _cumsum_lanes(x):
    return pl.pallas_call(
        _cumsum_kernel,
        out_shape=jax.ShapeDtypeStruct(x.shape, F32),
        name="cumsum_logf",
    )(x, _tri_ones("fwd_incl"))


def _attn_prologue(q_ref, v_ref, kvm_ref, qm_s, va_s, vb_s, vma_s, vmb_s, *, tq, n_real, ones_cols):
    lane512 = lax.broadcasted_iota(jnp.int32, (1, ATT_W), 1)
    even = ((lane512 >> 6) & 1) == 0

    def fill(dst_a, dst_b, rows, v):
        va = jnp.where(even, v, 0.0).astype(BF16)
        vb = jnp.where(even, 0.0, v).astype(BF16)
        if not ones_cols:
            dst_a[rows, :] = va
            dst_b[rows, :] = vb
            return
        ones = jnp.ones((v.shape[0], LANES), BF16)
        for p in range(N_HEADS // 2):
            src = slice(p * LANES, (p + 1) * LANES)
            for dst, val in ((dst_a, va), (dst_b, vb)):
                dst[rows, 2 * p * LANES:(2 * p + 1) * LANES] = val[:, src]
                dst[rows, (2 * p + 1) * LANES:(2 * p + 2) * LANES] = ones

    @pl.when(pl.program_id(1) == 0)
    def _():
        fill(vma_s, vmb_s, slice(0, PAGE), kvm_ref[:, ATT_W:].astype(F32))
        for r in range(n_real // ATT_BLK):
            rows = slice(r * ATT_BLK, (r + 1) * ATT_BLK)
            fill(va_s, vb_s, rows, v_ref[0, rows, :].astype(F32))

    first_head = lax.broadcasted_iota(jnp.int32, (tq, LANES), 1) < HEAD_DIM
    q = q_ref[0].astype(F32)
    for h in range(N_HEADS):
        qp = q[:, (h // 2) * LANES:(h // 2 + 1) * LANES]
        keep = first_head if h % 2 == 0 else jnp.logical_not(first_head)
        qm_s[h] = jnp.where(keep, qp, 0.0).astype(BF16)
    return first_head


def _fox_kernel(q_ref, k_ref, v_ref, kvm_ref, f_ref, o_ref,
                qm_s, va_s, vb_s, vma_s, vmb_s, m_s, l_s, acc_s, a_s, s_scr, p_scr,
                *, meta_only, tq, n_real):
    j = pl.program_id(1)
    first_head = _attn_prologue(q_ref, v_ref, kvm_ref, qm_s, va_s, vb_s, vma_s, vmb_s,
                                tq=tq, n_real=n_real, ones_cols=True)
    m_s[...] = jnp.full(m_s.shape, NEG, F32)
    l_s[...] = jnp.zeros(l_s.shape, F32)
    acc_s[...] = jnp.zeros(acc_s.shape, F32)
    diag_idx = 0 if meta_only else j + 1
    fref = [f_ref[0, h, pl.ds(diag_idx, 1), :][:, 0:1] for h in range(N_HEADS)]

    def block(kget, vaget, vbget, fidx, mask, tk):
        for p in range(N_HEADS // 2):
            k2 = kget(slice(p * LANES, (p + 1) * LANES))
            s2 = lax.dot_general(qm_s[2 * p:2 * p + 2].reshape(2 * tq, LANES), k2, _NT,
                                 preferred_element_type=F32)
            s_scr[2 * p, :, :tk] = s2[:tq]
            s_scr[2 * p + 1, :, :tk] = s2[tq:]

        def scores(h):
            frow = f_ref[0, h, pl.ds(fidx, 1), :][:, :tk]
            s = s_scr[h, :, :tk] - (frow - fref[h]) * LOG2E
            return s if mask is None else jnp.where(mask, s, NEG)

        for h in range(N_HEADS):
            m_old = m_s[h]
            m_new = jnp.maximum(m_old, jnp.max(scores(h), axis=1, keepdims=True))
            a_s[h] = jnp.exp2(m_old - m_new)
            m_s[h] = m_new
        for h in range(N_HEADS):
            m = m_s[h]
            m = m if tk == LANES else jnp.concatenate([m] * (tk // LANES), axis=1)
            p_scr[h, :, :tk] = jnp.exp2(scores(h) - m).astype(BF16)
        for p in range(N_HEADS // 2):
            aug = slice(2 * p * LANES, (2 * p + 2) * LANES)
            da = jnp.dot(p_scr[2 * p, :, :tk], vaget(aug), preferred_element_type=F32)
            db = jnp.dot(p_scr[2 * p + 1, :, :tk], vbget(aug), preferred_element_type=F32)
            l_s[2 * p] = a_s[2 * p] * l_s[2 * p] + da[:, LANES:]
            l_s[2 * p + 1] = a_s[2 * p + 1] * l_s[2 * p + 1] + db[:, LANES:]
            acc_s[p] = (jnp.where(first_head, a_s[2 * p], a_s[2 * p + 1]) * acc_s[p]
                        + da[:, :LANES] + db[:, :LANES])

    def meta_block(mask):
        block(lambda ln: kvm_ref[:, ln], lambda ln: vma_s[:, ln], lambda ln: vmb_s[:, ln],
              0, mask, PAGE)

    col_m = lax.broadcasted_iota(jnp.int32, (tq, PAGE), 1)
    if meta_only:
        row_m = lax.broadcasted_iota(jnp.int32, (tq, PAGE), 0)
        meta_block((col_m <= row_m) & (col_m < N_META))
    else:
        meta_block(col_m < N_META)

        def chunk(c, mask):
            rows = pl.ds(pl.multiple_of(c * ATT_BLK, ATT_BLK), ATT_BLK)
            block(lambda ln: k_ref[0, rows, ln], lambda ln: va_s[rows, ln],
                  lambda ln: vb_s[rows, ln], c + 1, mask, ATT_BLK)

        def body(c, carry):
            chunk(c, None)
            return carry

        lax.fori_loop(0, j, body, 0)
        row = lax.broadcasted_iota(jnp.int32, (tq, ATT_BLK), 0)
        col = lax.broadcasted_iota(jnp.int32, (tq, ATT_BLK), 1)
        chunk(j, col <= row)

    for p in range(N_HEADS // 2):
        l2 = jnp.where(first_head, l_s[2 * p], l_s[2 * p + 1])
        o_ref[0, :, p * LANES:(p + 1) * LANES] = (acc_s[p] / l2).astype(o_ref.dtype)


def _sb_kernel(q_ref, k_ref, v_ref, kvm_ref, tri_ref, trim_ref, o_ref,
               qm_s, va_s, vb_s, vma_s, vmb_s, r_s, acc_s, z_scr, hl_scr, p_scr,
               *, meta_only, tq, n_real):
    j = pl.program_id(1)
    _attn_prologue(q_ref, v_ref, kvm_ref, qm_s, va_s, vb_s, vma_s, vmb_s, tq=tq, n_real=n_real,
                   ones_cols=False)
    r_s[...] = jnp.zeros(r_s.shape, F32)
    acc_s[...] = jnp.zeros(acc_s.shape, F32)

    def block(kget, vaget, vbget, tri, mask, tk):
        for p in range(N_HEADS // 2):
            k2 = kget(slice(p * LANES, (p + 1) * LANES))
            z2 = lax.dot_general(qm_s[2 * p:2 * p + 2].reshape(2 * tq, LANES), k2, _NT,
                                 preferred_element_type=F32)
            z_scr[2 * p, :, :tk] = z2[:tq]
            z_scr[2 * p + 1, :, :tk] = z2[tq:]
        for h in range(N_HEADS):
            lg = _log2_sigmoid_neg(z_scr[h, :, :tk])
            if mask is not None:
                lg = jnp.where(mask, lg, 0.0)
            hi = lg.astype(BF16)
            hl_scr[h, :tq, :tk] = hi
            hl_scr[h, tq:, :tk] = (lg - hi.astype(F32)).astype(BF16)
        for h in range(N_HEADS):
            ct = (jnp.dot(hl_scr[h, :tq, :tk], tri, preferred_element_type=F32)
                  + jnp.dot(hl_scr[h, tq:, :tk], tri, preferred_element_type=F32))
            r = r_s[h]
            later = r if tk == LANES else jnp.concatenate([r] * (tk // LANES), axis=1)
            w = jnp.exp2(z_scr[h, :, :tk] + ct + later)
            if mask is not None:
                w = jnp.where(mask, w, 0.0)
            p_scr[h, :, :tk] = w.astype(BF16)
            r_s[h] = r + jnp.broadcast_to(ct[:, 0:1], (tq, LANES))
        for p in range(N_HEADS // 2):
            lanes = slice(p * LANES, (p + 1) * LANES)
            da = jnp.dot(p_scr[2 * p, :, :tk], vaget(lanes), preferred_element_type=F32)
            db = jnp.dot(p_scr[2 * p + 1, :, :tk], vbget(lanes), preferred_element_type=F32)
            acc_s[p] = acc_s[p] + da + db

    def meta_block(mask):
        block(lambda ln: kvm_ref[:, ln], lambda ln: vma_s[:, ln], lambda ln: vmb_s[:, ln],
              trim_ref[...], mask, PAGE)

    col_m = lax.broadcasted_iota(jnp.int32, (tq, PAGE), 1)
    if meta_only:
        row_m = lax.broadcasted_iota(jnp.int32, (tq, PAGE), 0)
        meta_block((col_m < row_m) & (col_m < N_META))
    else:
        def chunk(c, mask):
            rows = pl.ds(pl.multiple_of(c * ATT_BLK, ATT_BLK), ATT_BLK)
            block(lambda ln: k_ref[0, rows, ln], lambda ln: va_s[rows, ln],
                  lambda ln: vb_s[rows, ln], tri_ref[...], mask, ATT_BLK)

        row = lax.broadcasted_iota(jnp.int32, (tq, ATT_BLK), 0)
        col = lax.broadcasted_iota(jnp.int32, (tq, ATT_BLK), 1)
        chunk(j, col < row)

        def body(i, carry):
            chunk(j - 1 - i, None)
            return carry

        lax.fori_loop(0, j, body, 0)
        meta_block(col_m < N_META)

    for p in range(N_HEADS // 2):
        o_ref[0, :, p * LANES:(p + 1) * LANES] = acc_s[p].astype(o_ref.dtype)


def _prompt_attn(kind, q, kv, kv_meta, f_all=None, *, meta_only=False):
    bsz, n, _ = q.shape
    tq = PAGE if meta_only else ATT_BLK
    idx3 = lambda c: (lambda b, j: (b, 0, c))
    in_specs = [pl.BlockSpec((1, tq, ATT_W), lambda b, j: (b, j, 0)),
                pl.BlockSpec((1, n, ATT_W), idx3(0)),
                pl.BlockSpec((1, n, ATT_W), idx3(1)),
                pl.BlockSpec((PAGE, 2 * ATT_W), lambda b, j: (0, 0))]
    vw = ATT_W * (2 if kind == "fox" else 1)
    scratch = [pltpu.VMEM((N_HEADS, tq, LANES), BF16),
               pltpu.VMEM((n, vw), BF16), pltpu.VMEM((n, vw), BF16),
               pltpu.VMEM((PAGE, vw), BF16), pltpu.VMEM((PAGE, vw), BF16)]
    acc = pltpu.VMEM((N_HEADS // 2, tq, LANES), F32)
    if kind == "fox":
        kernel = _fox_kernel
        extra = [f_all]
        in_specs.append(pl.BlockSpec((1, N_HEADS) + f_all.shape[2:], lambda b, j: (b, 0, 0, 0)))
        scratch += [pltpu.VMEM((N_HEADS, tq, LANES), F32), pltpu.VMEM((N_HEADS, tq, LANES), F32), acc,
                    pltpu.VMEM((N_HEADS, tq, LANES), F32),
                    pltpu.VMEM((N_HEADS, tq, ATT_BLK), F32), pltpu.VMEM((N_HEADS, tq, ATT_BLK), BF16)]
    else:
        kernel = _sb_kernel
        extra = [_tri_ones("rev_incl", ATT_BLK)[:, :ATT_BLK], _tri_ones("rev_incl", PAGE)[:, :PAGE]]
        in_specs += [pl.BlockSpec(e.shape, lambda b, j: (0, 0)) for e in extra]
        scratch += [pltpu.VMEM((N_HEADS, tq, LANES), F32), acc,
                    pltpu.VMEM((N_HEADS, tq, ATT_BLK), F32),
                    pltpu.VMEM((N_HEADS, 2 * tq, ATT_BLK), BF16),
                    pltpu.VMEM((N_HEADS, tq, ATT_BLK), BF16)]
    return pl.pallas_call(
        functools.partial(kernel, meta_only=meta_only, tq=tq, n_real=0 if meta_only else n),
        grid=(bsz, n // tq),
        in_specs=in_specs,
        out_specs=pl.BlockSpec((1, tq, ATT_W), lambda b, j: (b, j, 0)),
        out_shape=jax.ShapeDtypeStruct((bsz, n, ATT_W), BF16),
        scratch_shapes=scratch,
        compiler_params=_cparams(2),
        name=kind + ("_meta_attn" if meta_only else "_prompt_attn"),
    )(q, kv, kv, kv_meta, *extra)


def _own_head_mask():
    sub = lax.broadcasted_iota(jnp.int32, (N_HEADS, ATT_W), 0)
    lane = lax.broadcasted_iota(jnp.int32, (N_HEADS, ATT_W), 1)
    return lax.shift_right_logical(lane, 6) == sub


def _broadcast_q_columns(q_row, qb_scr):
    r = lax.broadcasted_iota(jnp.int32, (ATT_W, ATT_W), 0)
    c = lax.broadcasted_iota(jnp.int32, (ATT_W, ATT_W), 1)
    qd = jnp.where(r == c, jnp.broadcast_to(q_row, (ATT_W, ATT_W)), 0.0).astype(BF16)
    qb_scr[...] = jnp.dot(qd, jnp.ones((ATT_W, LANES), BF16), preferred_element_type=F32)


def _page_scores(qb_scr, kt_ref):
    sub = lax.broadcasted_iota(jnp.int32, (N_HEADS, PAGE), 0)
    s = jnp.zeros((N_HEADS, PAGE), F32)
    for h in range(N_HEADS):
        t = qb_scr[h * HEAD_DIM:(h + 1) * HEAD_DIM, :] * kt_ref[h]
        s = jnp.where(sub == h, jnp.broadcast_to(jnp.sum(t, axis=0, keepdims=True), (N_HEADS, PAGE)), s)
    return s


def _add_page_values(acc_scr, w, vt_ref):
    for h in range(N_HEADS):
        rows = slice(h * HEAD_DIM, (h + 1) * HEAD_DIM)
        acc_scr[rows, :] = acc_scr[rows, :] + vt_ref[h] * jnp.broadcast_to(w[h:h + 1, :], (HEAD_DIM, PAGE))


def _lane_sums_as_row(acc_scr):
    ones = jnp.ones((SUBLANES, LANES), BF16)
    out = None
    for part in _split_bf16(acc_scr[...], 3):
        d = lax.dot_general(ones, part, _NT, preferred_element_type=F32)
        out = d if out is None else out + d
    return out[0:1, :]


def _sample_fox_kernel(pt_ref, q_ref, kown_ref, vown_ref, lfown_ref, tri_ref, *refs):
    del pt_ref
    kt, vt, lf = refs[:N_PAGES], refs[N_PAGES:2 * N_PAGES], refs[2 * N_PAGES:3 * N_PAGES]
    o_ref, qb_scr, acc_scr = refs[3 * N_PAGES:]
    q_row = q_ref[0].astype(F32)
    _broadcast_q_columns(q_row, qb_scr)
    own_head = _own_head_mask()
    carry = jnp.broadcast_to(lfown_ref[0], (N_HEADS, PAGE))
    scores = [None] * N_PAGES
    for p in reversed(range(N_PAGES)):
        ct = _dot01(lf[p][...], tri_ref[...])
        scores[p] = _page_scores(qb_scr, kt[p]) + (ct[:, :PAGE] + carry) * LOG2E
        carry = carry + ct[:, PAGE:]
    q_bd = jnp.where(own_head, jnp.broadcast_to(q_row, (N_HEADS, ATT_W)), 0.0)
    s_own = jnp.sum(q_bd * kown_ref[0], axis=1, keepdims=True)
    m = s_own
    for p in range(N_PAGES):
        m = jnp.maximum(m, jnp.max(scores[p], axis=1, keepdims=True))
    e_own = jnp.exp2(s_own - m)
    l = e_own
    acc_scr[...] = jnp.zeros(acc_scr.shape, F32)
    for p in range(N_PAGES):
        e = jnp.exp2(scores[p] - m)
        l = l + jnp.sum(e, axis=1, keepdims=True)
        _add_page_values(acc_scr, e, vt[p])
    own_row = jnp.sum(jnp.where(own_head, e_own * vown_ref[0], 0.0), axis=0, keepdims=True)
    l_row = jnp.sum(jnp.where(own_head, jnp.broadcast_to(l, (N_HEADS, ATT_W)), 0.0), axis=0, keepdims=True)
    o_ref[0] = ((_lane_sums_as_row(acc_scr) + own_row) / l_row).astype(o_ref.dtype)


def _sample_sb_kernel(pt_ref, q_ref, tri_ref, *refs):
    del pt_ref
    kt, vt = refs[:N_PAGES], refs[N_PAGES:2 * N_PAGES]
    o_ref, qb_scr, acc_scr = refs[2 * N_PAGES:]
    _broadcast_q_columns(q_ref[0].astype(F32), qb_scr)
    acc_scr[...] = jnp.zeros(acc_scr.shape, F32)
    carry = jnp.zeros((N_HEADS, PAGE), F32)
    for p in reversed(range(N_PAGES)):
        z = _page_scores(qb_scr, kt[p])
        ct = _dot01(_log2_sigmoid_neg(z), tri_ref[...])
        _add_page_values(acc_scr, jnp.exp2(z + ct[:, :PAGE] + carry), vt[p])
        carry = carry + ct[:, PAGE:]
    o_ref[0] = _lane_sums_as_row(acc_scr).astype(o_ref.dtype)


def _sample_attn(kind, layer, page_table, q, cache_kt, cache_vt, *, k_own=None, v_own=None, lf_own=None,
                 cache_lf_t=None):
    db = q.shape[0]
    row3 = pl.BlockSpec((1, 1, ATT_W), lambda b, pt: (b, 0, 0))
    tri_spec = pl.BlockSpec((LANES, 2 * LANES), lambda b, pt: (0, 0))

    def pages(block):
        zeros = (0,) * (len(block) - 2)
        return [pl.BlockSpec(block, lambda b, pt, p=p: (layer, pt[b * N_PAGES + p]) + zeros)
                for p in range(N_PAGES)]

    kv_block = (None, None, N_HEADS, HEAD_DIM, PAGE)
    scratch = [pltpu.VMEM((ATT_W, LANES), F32), pltpu.VMEM((ATT_W, PAGE), F32)]
    if kind == "fox":
        kernel = _sample_fox_kernel
        in_specs = ([row3, row3, row3, pl.BlockSpec((1, N_HEADS, 1), lambda b, pt: (b, 0, 0)), tri_spec]
                    + pages(kv_block) + pages(kv_block) + pages((None, None, N_HEADS, PAGE)))
        args = ([q, k_own, v_own, lf_own, _tri_ones("rev_excl")]
                + [cache_kt] * N_PAGES + [cache_vt] * N_PAGES + [cache_lf_t] * N_PAGES)
    else:
        kernel = _sample_sb_kernel
        in_specs = [row3, tri_spec] + pages(kv_block) + pages(kv_block)
        args = [q, _tri_ones("rev_incl")] + [cache_kt] * N_PAGES + [cache_vt] * N_PAGES
    return pl.pallas_call(
        kernel,
        grid_spec=pltpu.PrefetchScalarGridSpec(
            num_scalar_prefetch=1, grid=(db,), in_specs=in_specs, out_specs=row3,
            scratch_shapes=scratch),
        out_shape=jax.ShapeDtypeStruct((db, 1, ATT_W), BF16),
        compiler_params=_cparams(1),
        name="sample_" + kind,
    )(page_table.reshape(-1), *args)


def _block_diag_ssm_weights(bb_re_t, bb_im_t, c_re, c_im):
    eye = jnp.eye(8, dtype=F32)

    def b_blocks(bt):
        b = bt.reshape(SSM_GROUP, 4, 8, SSM_STATE)
        return jnp.einsum("cjgp,gh->jgchp", b, eye).reshape(4, LANES, STATE_W // 4)

    def c_blocks(c):
        c4 = c.reshape(4, 8, SSM_GROUP, SSM_STATE)
        return jnp.einsum("jgcp,gh->jgphc", c4, eye).reshape(4, STATE_W // 4, LANES)

    wb = jnp.concatenate([b_blocks(bb_re_t), b_blocks(bb_im_t)], axis=2).astype(BF16)
    wc = jnp.concatenate([c_blocks(c_re), -c_blocks(c_im)], axis=1).astype(BF16)
    return wb, wc


def kernel(x_prompt, x_sample, cache_fox_k, cache_fox_v, cache_fox_logf, cache_sb_k, cache_sb_v,
           state_ssm_re, state_ssm_im, page_table, meta_tokens, ln_in_g, ln_in_b, w_in, b_forget,
           ssm_a_re, ssm_a_im, ssm_log_dt, ssm_b_re, ssm_b_im, ssm_c_re, ssm_c_im, ssm_d, w_ssm_glu,
           w_fox_o, w_sb_o, w_out, ln1_g, ln1_b, w_ffn_in, w_ffn_out, ln2_g, ln2_b):
    bsz, seq, d = x_prompt.shape
    db = x_sample.shape[0]
    depth = w_in.shape[0]
    d_ff = w_ffn_out.shape[1]
    n_small = db + N_META
    alpha = (2 * depth) ** 0.25
    n_chunks = seq // ATT_BLK

    off_f = IN_F
    off_sq = off_f + N_HEADS

    flat = lambda a: a.reshape(depth, STATE_W)
    ldt_rep = jnp.repeat(ssm_log_dt, SSM_STATE, axis=1)
    b_t = lambda a: a.reshape(depth, STATE_W, SSM_GROUP).transpose(0, 2, 1)
    abr, abi, bbr_t, bbi_t = pl.pallas_call(
        _zoh_kernel,
        out_shape=[jax.ShapeDtypeStruct((depth, STATE_W), F32)] * 2
        + [jax.ShapeDtypeStruct((depth, SSM_GROUP, STATE_W), F32)] * 2,
        name="zoh",
    )(flat(ssm_a_re), flat(ssm_a_im), ldt_rep, b_t(ssm_b_re), b_t(ssm_b_im))

    w_in16 = w_in.astype(BF16)
    w_glu16 = w_ssm_glu.astype(BF16)
    w_fox_o16 = w_fox_o.astype(BF16)
    w_sb_o16 = w_sb_o.astype(BF16)
    w_out16 = w_out.astype(BF16)
    w_ffn_in16 = w_ffn_in.astype(BF16)
    w_ffn_out16 = w_ffn_out.astype(BF16)

    pos_minor = lambda c: c.transpose(0, 1, 3, 4, 2)
    cache_fox_kt, cache_fox_vt = pos_minor(cache_fox_k), pos_minor(cache_fox_v)
    cache_sb_kt, cache_sb_vt = pos_minor(cache_sb_k), pos_minor(cache_sb_v)
    cache_lf_t = cache_fox_logf.astype(F32).transpose(0, 1, 3, 2)

    tm_big = 512
    h_big = _ln_rows(x_prompt.reshape(bsz * seq, d), ln_in_g, ln_in_b, tm_big)
    x_small = jnp.concatenate([x_sample.reshape(db, d), meta_tokens.astype(x_prompt.dtype)], axis=0)
    h_small = _ln_rows(x_small, ln_in_g, ln_in_b, n_small)

    outs = {k: [] for k in ("pfk", "pfv", "plf", "psk", "psv", "phr", "phi",
                            "sfk", "sfv", "slf", "ssk", "ssv", "shr", "shi")}

    for l in range(depth):
        pad_f = ((0, 0), (0, LANES - N_HEADS))
        wl = jnp.concatenate([w_in16[l][:, :off_f], jnp.pad(w_in16[l][:, off_f:off_sq], pad_f),
                              w_in16[l][:, off_sq:]], axis=1)
        b_f = jnp.pad(b_forget[l].reshape(1, N_HEADS), pad_f)
        wb, wc = _block_diag_ssm_weights(bbr_t[l], bbi_t[l], ssm_c_re[l], ssm_c_im[l])
        dvec = ssm_d[l].reshape(1, SSM_W)
        wg = (w_glu16[l][:, :d], w_glu16[l][:, d:])
        ar_l, ai_l = abr[l:l + 1], abi[l:l + 1]

        def in_proj(h, tm, nb):
            u, fq, fkv32, fkv16, lf, sq, skv32, skv16, gates = _in_proj(h, wl, b_f, tm=tm, nb=nb)
            return u, fq, fkv32, fkv16, lf[:, :N_HEADS], sq, skv32, skv16, gates

        u_s, fq_s, fkv32_s, fkv16_s, lf_s, sq_s, skv32_s, skv16_s, gates_s = in_proj(h_small, n_small, None)

        y_smp, shr, shi = _ssm(u_s[:db], state_ssm_re[l].reshape(db, STATE_W),
                               state_ssm_im[l].reshape(db, STATE_W), ar_l, ai_l, wb, wc, dvec, wg,
                               nb=db, steps_per_chunk=1)
        zeros8 = jnp.zeros((SUBLANES, STATE_W), F32)
        y_meta, mhr, mhi = _ssm(jnp.repeat(u_s[db:], SUBLANES, axis=0), zeros8, zeros8,
                                ar_l, ai_l, wb, wc, dvec, wg, nb=SUBLANES, steps_per_chunk=N_META)
        y_small = jnp.concatenate([y_smp, y_meta[::SUBLANES]], axis=0)

        u_b, fq_b, fkv32_b, fkv16_b, lf_b, sq_b, skv32_b, skv16_b, gates_b = in_proj(h_big, ATT_BLK, bsz)

        lf_meta_t = jnp.pad(lf_s[db:].T, ((0, 0), (0, ATT_BLK - N_META)))
        lf_real_t = lf_b.reshape(bsz, seq, N_HEADS).transpose(0, 2, 1)
        lf_all = jnp.concatenate(
            [jnp.broadcast_to(lf_meta_t[None], (bsz, N_HEADS, ATT_BLK)), lf_real_t], axis=2)
        f_all = _cumsum_lanes(lf_all.reshape(bsz * N_HEADS, ATT_BLK + seq))
        f_all = f_all.reshape(bsz, N_HEADS, n_chunks + 1, ATT_BLK)

        pad_meta = lambda a: jnp.pad(a[db:], ((0, PAGE - N_META), (0, 0)))
        fkv_meta = pad_meta(fkv16_s)
        skv_meta = pad_meta(skv16_s)

        a_fox_meta = _prompt_attn("fox", pad_meta(fq_s)[None], fkv_meta[None], fkv_meta,
                                  f_all[:1], meta_only=True)[0, :N_META]
        a_sb_meta = _prompt_attn("sb", pad_meta(sq_s)[None], skv_meta[None], skv_meta,
                                 meta_only=True)[0, :N_META]

        row3 = lambda a: a.reshape(db, 1, ATT_W)
        a_fox_smp = _sample_attn(
            "fox", l, page_table, row3(fq_s[:db]), cache_fox_kt, cache_fox_vt,
            k_own=row3(fkv32_s[:db, :ATT_W]), v_own=row3(fkv32_s[:db, ATT_W:]),
            lf_own=lf_s[:db].reshape(db, N_HEADS, 1), cache_lf_t=cache_lf_t)
        a_sb_smp = _sample_attn("sb", l, page_table, row3(sq_s[:db]), cache_sb_kt, cache_sb_vt)
        a_fox_small = jnp.concatenate([a_fox_smp.reshape(db, ATT_W), a_fox_meta], axis=0)
        a_sb_small = jnp.concatenate([a_sb_smp.reshape(db, ATT_W), a_sb_meta], axis=0)

        u_b = u_b.reshape(seq * bsz, SSM_W)
        y_big, phr, phi = _ssm(u_b, mhr, mhi, ar_l, ai_l, wb, wc, dvec, wg, nb=bsz, steps_per_chunk=64)
        a_fox_big = _prompt_attn("fox", fq_b.reshape(bsz, seq, ATT_W),
                                 fkv16_b.reshape(bsz, seq, 2 * ATT_W), fkv_meta, f_all)
        a_sb_big = _prompt_attn("sb", sq_b.reshape(bsz, seq, ATT_W),
                                skv16_b.reshape(bsz, seq, 2 * ATT_W), skv_meta)

        def tail(h, y, a_fox, a_sb, gates, nb, tm):
            m = h.shape[0]
            glu = y
            h1 = _mix_ln(gates, glu.reshape(m // nb, nb * d), a_fox, a_sb, w_fox_o16[l], w_sb_o16[l],
                         w_out16[l], h, ln1_g[l], ln1_b[l], nb=nb, tm=tm, alpha=alpha)
            act = _gated(h1, w_ffn_in16[l][:, :d_ff], w_ffn_in16[l][:, d_ff:], BF16, tm=tm,
                         tn=d_ff // 2, silu_a=True, name="ffn_in")
            return _proj_res_ln(act, w_ffn_out16[l], h1, ln2_g[l], ln2_b[l], tm=tm, alpha=alpha)

        h_small = tail(h_small, y_small, a_fox_small, a_sb_small, gates_s, 1, n_small)
        h_big = tail(h_big, y_big, a_fox_big.reshape(bsz * seq, ATT_W),
                     a_sb_big.reshape(bsz * seq, ATT_W), gates_b, bsz, tm_big)

        def with_meta(real, meta):
            w = real.shape[1]
            return jnp.concatenate(
                [jnp.broadcast_to(meta[None], (bsz, N_META, w)), real.reshape(bsz, seq, w)], axis=1)

        def with_meta_t(real_t, meta):
            full = jnp.concatenate(
                [jnp.broadcast_to(meta.T[None], (bsz, ATT_W, N_META)), real_t], axis=2)
            return full.reshape(bsz, N_HEADS, HEAD_DIM, N_META + seq).transpose(0, 3, 1, 2)

        heads = lambda a: a.reshape(a.shape[:-1] + (N_HEADS, HEAD_DIM))
        outs["pfk"].append(with_meta_t(fkv32_b[:, :ATT_W], fkv32_s[db:, :ATT_W]))
        outs["pfv"].append(with_meta_t(fkv32_b[:, ATT_W:], fkv32_s[db:, ATT_W:]))
        outs["plf"].append(with_meta(lf_b, lf_s[db:]))
        outs["psk"].append(with_meta_t(skv32_b[:, :ATT_W], skv32_s[db:, :ATT_W]))
        outs["psv"].append(with_meta_t(skv32_b[:, ATT_W:], skv32_s[db:, ATT_W:]))
        outs["phr"].append(phr.reshape(bsz, N_GROUPS, SSM_STATE))
        outs["phi"].append(phi.reshape(bsz, N_GROUPS, SSM_STATE))
        outs["sfk"].append(heads(fkv32_s[:db, :ATT_W].reshape(db, 1, ATT_W)))
        outs["sfv"].append(heads(fkv32_s[:db, ATT_W:].reshape(db, 1, ATT_W)))
        outs["slf"].append(lf_s[:db].reshape(db, 1, N_HEADS))
        outs["ssk"].append(heads(skv32_s[:db, :ATT_W].reshape(db, 1, ATT_W)))
        outs["ssv"].append(heads(skv32_s[:db, ATT_W:].reshape(db, 1, ATT_W)))
        outs["shr"].append(shr.reshape(db, N_GROUPS, SSM_STATE))
        outs["shi"].append(shi.reshape(db, N_GROUPS, SSM_STATE))

    st = {k: jnp.stack(v) for k, v in outs.items()}
    return (h_big.reshape(bsz, seq, d), h_small[:db].reshape(db, 1, d),
            st["pfk"], st["pfv"], st["plf"], st["psk"], st["psv"], st["phr"], st["phi"],
            st["sfk"], st["sfv"], st["slf"], st["ssk"], st["ssv"], st["shr"], st["shi"])
```
